```python
import jax, jax.numpy as jnp
from jax import lax
import numpy as np

D_MODEL = 1024
BATCH = 8
SEQ = 2048
DEPTH = 4

N_MIXERS = 3
EPS = 1e-6
CHUNK = 128
E_A = 2 * D_MODEL
G_A = 8
E_B = 3 * D_MODEL // 2
H_B = 16
BD_B = E_B // H_B
CONV_W = 4
LRU_C = 8.0
H_C = 16
DH_C = D_MODEL // H_C
E_C = H_C * DH_C
Q_BLOCK = 128

kernel_name = "hybrid_gmlp_rglru_fox_trunk"


def rms_norm(x, g):
    xf = x.astype(jnp.float32)
    y = xf * lax.rsqrt(jnp.mean(xf * xf, axis=-1, keepdims=True) + EPS)
    return (y * g.astype(jnp.float32)).astype(x.dtype)


def gmlp_mixer(h, w_in, v_norm, w_s, b_s, w_out):
    B, S, _ = h.shape
    u, v, g = jnp.split(h @ w_in, 3, axis=-1)
    u = jax.nn.gelu(u)
    v = rms_norm(jax.nn.gelu(v), v_norm)
    vc = v.reshape(B, S // CHUNK, CHUNK, G_A, E_A // G_A)
    causal = jnp.tril(jnp.ones((CHUNK, CHUNK), dtype=bool))
    w = jnp.where(causal[None], w_s, 0.0).astype(v.dtype)
    mixed = jnp.einsum('gts,bcsgd->bctgd', w, vc) + b_s.T[None, None, :, :, None].astype(v.dtype)
    y = u * mixed.reshape(B, S, E_A) * jax.nn.silu(g)
    return y @ w_out


def _lru_combine(left, right):
    a_l, b_l = left
    a_r, b_r = right
    return a_l * a_r, a_r * b_l + b_r


def rglru_mixer(h, w_in, conv_w, conv_b, w_a, b_a, w_x, b_x, lam, w_out):
    B, S, _ = h.shape
    xb, g = jnp.split(h @ w_in, 2, axis=-1)
    xc = lax.conv_general_dilated(
        xb, conv_w[:, None, :].astype(xb.dtype), window_strides=(1,), padding=[(CONV_W - 1, 0)],
        dimension_numbers=('NWC', 'WIO', 'NWC'), feature_group_count=E_B) + conv_b.astype(xb.dtype)
    xh = xc.reshape(B, S, H_B, BD_B)
    r = jax.nn.sigmoid((jnp.einsum('bshi,hij->bshj', xh, w_a).reshape(B, S, E_B) + b_a).astype(jnp.float32))
    i = jax.nn.sigmoid((jnp.einsum('bshi,hij->bshj', xh, w_x).reshape(B, S, E_B) + b_x).astype(jnp.float32))
    log_a = -LRU_C * r * jax.nn.softplus(-lam.astype(jnp.float32))
    a = jnp.exp(log_a)
    inp = jnp.sqrt(-jnp.expm1(2.0 * log_a)) * (i * xc.astype(jnp.float32))
    _, hs = lax.associative_scan(_lru_combine, (a, inp), axis=1)
    y = hs.astype(h.dtype) * jax.nn.silu(g)
    return y @ w_out


def fox_mixer(h, w_in, b_f, q_norm, k_norm, w_out):
    B, S, _ = h.shape
    q, k, v, g, f_logit = jnp.split(h @ w_in, [E_C, 2 * E_C, 3 * E_C, 4 * E_C], axis=-1)
    q = rms_norm(q.reshape(B, S, H_C, DH_C), q_norm)
    k = rms_norm(k.reshape(B, S, H_C, DH_C), k_norm)
    v = v.reshape(B, S, H_C, DH_C)
    log_f = jax.nn.log_sigmoid(f_logit.astype(jnp.float32) + b_f.astype(jnp.float32))
    cum = jnp.cumsum(log_f, axis=1).transpose(0, 2, 1)
    scale = DH_C ** -0.5
    outs = []
    for q0 in range(0, S, Q_BLOCK):
        end = q0 + Q_BLOCK
        s = jnp.einsum('bqhd,bkhd->bhqk', q[:, q0:end], k[:, :end]).astype(jnp.float32) * scale
        s = s + cum[:, :, q0:end, None] - cum[:, :, None, :end]
        qi = jnp.arange(q0, end)[:, None]
        ki = jnp.arange(end)[None, :]
        s = jnp.where(ki <= qi, s, -jnp.inf)
        p = jax.nn.softmax(s, axis=-1).astype(v.dtype)
        outs.append(jnp.einsum('bhqk,bkhd->bqhd', p, v[:, :end]))
    o = jnp.concatenate(outs, axis=1).reshape(B, S, E_C)
    y = o * jax.nn.silu(g)
    return y @ w_out


def _nrm(k, shape, scale):
    return jax.random.normal(k, shape, jnp.float32) * scale


def _init_gmlp(key, p):
    k = jax.random.split(key, 6)
    return {
        p + 'norm': 1.0 + _nrm(k[0], (D_MODEL,), 0.02),
        p + 'w_in': _nrm(k[1], (D_MODEL, 3 * E_A), D_MODEL ** -0.5),
        p + 'v_norm': 1.0 + _nrm(k[2], (E_A,), 0.02),
        p + 'w_s': _nrm(k[3], (G_A, CHUNK, CHUNK), 0.5 * CHUNK ** -0.5),
        p + 'b_s': 1.0 + _nrm(k[4], (G_A, CHUNK), 0.02),
        p + 'w_out': _nrm(k[5], (E_A, D_MODEL), E_A ** -0.5),
    }


def _init_rglru(key, p):
    k = jax.random.split(key, 10)
    a0 = jax.random.uniform(k[8], (E_B,), jnp.float32, 0.9, 0.999)
    s0 = a0 ** (1.0 / LRU_C)
    return {
        p + 'norm': 1.0 + _nrm(k[0], (D_MODEL,), 0.02),
        p + 'w_in': _nrm(k[1], (D_MODEL, 2 * E_B), D_MODEL ** -0.5),
        p + 'conv_w': _nrm(k[2], (CONV_W, E_B), CONV_W ** -0.5),
        p + 'conv_b': _nrm(k[3], (E_B,), 0.02),
        p + 'w_a': _nrm(k[4], (H_B, BD_B, BD_B), BD_B ** -0.5),
        p + 'b_a': _nrm(k[5], (E_B,), 0.02),
        p + 'w_x': _nrm(k[6], (H_B, BD_B, BD_B), BD_B ** -0.5),
        p + 'b_x': _nrm(k[7], (E_B,), 0.02),
        p + 'lam': jnp.log(s0) - jnp.log1p(-s0),
        p + 'w_out': _nrm(k[9], (E_B, D_MODEL), E_B ** -0.5),
    }


def _init_fox(key, p):
    k = jax.random.split(key, 6)
    return {
        p + 'norm': 1.0 + _nrm(k[0], (D_MODEL,), 0.02),
        p + 'w_in': _nrm(k[1], (D_MODEL, 4 * E_C + H_C), D_MODEL ** -0.5),
        p + 'b_f': jax.random.uniform(k[2], (H_C,), jnp.float32, 1.0, 4.0),
        p + 'q_norm': 1.0 + _nrm(k[3], (DH_C,), 0.02),
        p + 'k_norm': 1.0 + _nrm(k[4], (DH_C,), 0.02),
        p + 'w_out': _nrm(k[5], (E_C, D_MODEL), E_C ** -0.5),
    }


def setup_inputs(seed: int = 0) -> dict:
    key = jax.random.key(seed)
    keys = jax.random.split(key, DEPTH + 1)
    inputs = {'x': jax.random.normal(keys[0], (BATCH, SEQ, D_MODEL), jnp.float32)}
    builders = (_init_gmlp, _init_rglru, _init_fox)
    for i in range(DEPTH):
        inputs.update(builders[i % N_MIXERS](keys[i + 1], 'l%d_' % i))
    return inputs


def reference(x,
              l0_norm, l0_w_in, l0_v_norm, l0_w_s, l0_b_s, l0_w_out,
              l1_norm, l1_w_in, l1_conv_w, l1_conv_b, l1_w_a, l1_b_a, l1_w_x, l1_b_x, l1_lam, l1_w_out,
              l2_norm, l2_w_in, l2_b_f, l2_q_norm, l2_k_norm, l2_w_out,
              l3_norm, l3_w_in, l3_v_norm, l3_w_s, l3_b_s, l3_w_out):
    layers = [
        (l0_norm, (l0_w_in, l0_v_norm, l0_w_s, l0_b_s, l0_w_out)),
        (l1_norm, (l1_w_in, l1_conv_w, l1_conv_b, l1_w_a, l1_b_a, l1_w_x, l1_b_x, l1_lam, l1_w_out)),
        (l2_norm, (l2_w_in, l2_b_f, l2_q_norm, l2_k_norm, l2_w_out)),
        (l3_norm, (l3_w_in, l3_v_norm, l3_w_s, l3_b_s, l3_w_out)),
    ]
    mixers = (gmlp_mixer, rglru_mixer, fox_mixer)
    for i in range(DEPTH):
        norm, params = layers[i]
        x = x + mixers[i % N_MIXERS](rms_norm(x, norm), *params).astype(x.dtype)
    return x
```

```python
import functools

import jax
import jax.numpy as jnp
from jax import lax
from jax.experimental import pallas as pl
from jax.experimental.pallas import tpu as pltpu

D_MODEL = 1024
BATCH = 8
SEQ = 2048
EPS = 1e-6
CHUNK = 128
E_A = 2 * D_MODEL
G_A = 8
DG_A = E_A // G_A
E_B = 3 * D_MODEL // 2
H_B = 16
BD_B = E_B // H_B
CONV_W = 4
LRU_C = 8.0
H_C = 16
DH_C = D_MODEL // H_C
E_C = H_C * DH_C

LANES = 128
SUBLANES = 8
VMEM_LIMIT = 56 * 1024 * 1024

TM_A = 512
T_B = 64
HG_B = 4
GW_B = HG_B * BD_B
TM_C = 512
TQ_C = 256
TK_C = 256
NEG_BIG = -1e30

F32 = jnp.float32
BF16 = jnp.bfloat16


def _dot(a, b):
    return jnp.dot(a, b, preferred_element_type=F32)


def _rms(x, g):
    ms = jnp.mean(x * x, axis=-1, keepdims=True)
    return x * lax.rsqrt(ms + EPS) * g


def _gelu(x):
    return x * (0.5 * (1.0 + jnp.tanh(0.7978845608028654 * (x + 0.044715 * (x * x * x)))))


def _sigmoid(x):
    return 0.5 * (1.0 + jnp.tanh(0.5 * x))


def _silu(x):
    return x * _sigmoid(x)


def _softplus(x):
    return jnp.maximum(x, 0.0) + jnp.log1p(jnp.exp(-jnp.abs(x)))


def _const_spec(shape):
    n = len(shape)
    return pl.BlockSpec(shape, lambda *_: (0,) * n, pipeline_mode=pl.Buffered(1))


def _gmlp_kernel(x_ref, nrm_ref, win_ref, vnrm_ref, ws_ref, bst_ref, wo_ref, o_ref, vn_scr, y_scr):
    x = x_ref[...]
    h = _rms(x, nrm_ref[...]).astype(BF16)
    v = _gelu(_dot(h, win_ref[:, E_A:2 * E_A]))
    vn_scr[...] = _rms(v, vnrm_ref[...]).astype(BF16)
    row = lax.broadcasted_iota(jnp.int32, (CHUNK, CHUNK), 0)
    col = lax.broadcasted_iota(jnp.int32, (CHUNK, CHUNK), 1)
    causal = col <= row
    for g in range(G_A):
        c0 = g * DG_A
        u = _gelu(_dot(h, win_ref[:, c0:c0 + DG_A]))
        gate = _silu(_dot(h, win_ref[:, 2 * E_A + c0:2 * E_A + c0 + DG_A]))
        w = jnp.where(causal, ws_ref[g], 0.0).astype(BF16)
        bias = bst_ref[:, g:g + 1]
        for c in range(TM_A // CHUNK):
            r0 = c * CHUNK
            mixed = _dot(w, vn_scr[r0:r0 + CHUNK, c0:c0 + DG_A]) + bias
            y = u[r0:r0 + CHUNK] * mixed * gate[r0:r0 + CHUNK]
            y_scr[r0:r0 + CHUNK, c0:c0 + DG_A] = y.astype(BF16)
    o_ref[...] = x + _dot(y_scr[...], wo_ref[...])


def _gmlp_layer(x2, norm, w_in, v_norm, w_s, b_s, w_out):
    n = x2.shape[0]
    return pl.pallas_call(
        _gmlp_kernel,
        grid=(n // TM_A,),
        in_specs=[
            pl.BlockSpec((TM_A, D_MODEL), lambda i: (i, 0)),
            _const_spec((1, D_MODEL)),
            _const_spec((D_MODEL, 3 * E_A)),
            _const_spec((1, E_A)),
            _const_spec((G_A, CHUNK, CHUNK)),
            _const_spec((CHUNK, G_A)),
            _const_spec((E_A, D_MODEL)),
        ],
        out_specs=pl.BlockSpec((TM_A, D_MODEL), lambda i: (i, 0)),
        out_shape=jax.ShapeDtypeStruct((n, D_MODEL), F32),
        scratch_shapes=[pltpu.VMEM((TM_A, E_A), BF16), pltpu.VMEM((TM_A, E_A), BF16)],
        compiler_params=pltpu.CompilerParams(dimension_semantics=("arbitrary",), vmem_limit_bytes=VMEM_LIMIT),
        name="gmlp_layer",
    )(x2, norm.reshape(1, D_MODEL), w_in.astype(BF16), v_norm.reshape(1, E_A), w_s, b_s.T, w_out.astype(BF16))


ROWS_B = T_B * BATCH
TAIL_B = (CONV_W - 1) * BATCH


def _rglru_kernel(x_ref, nrm_ref, win_ref, cw_ref, cb_ref, wbd_ref, ba_ref, bx_ref, lam_ref, wo_ref, o_ref,
                  hn_scr, xb_scr, a_scr, hs_scr, state_scr, out_scr):
    @pl.when(pl.program_id(0) == 0)
    def _():
        xb_scr[0:TAIL_B, :] = jnp.zeros((TAIL_B, E_B), F32)
        state_scr[...] = jnp.zeros((BATCH, E_B), F32)

    nrm = nrm_ref[...]
    for b in range(BATCH):
        hb = _rms(x_ref[b], nrm)
        for c in range(D_MODEL // LANES):
            hn_scr[c, pl.ds(b, T_B, stride=BATCH), :] = hb[:, c * LANES:(c + 1) * LANES]
    h = jnp.concatenate([hn_scr[c] for c in range(D_MODEL // LANES)], axis=1).astype(BF16)
    xb_scr[TAIL_B:TAIL_B + ROWS_B, :] = _dot(h, win_ref[:, 0:E_B])
    gate = _silu(_dot(h, win_ref[:, E_B:2 * E_B]))

    xc = cb_ref[...] + cw_ref[0:1, :] * xb_scr[0:ROWS_B, :]
    for k in range(1, CONV_W):
        xc = xc + cw_ref[k:k + 1, :] * xb_scr[k * BATCH:k * BATCH + ROWS_B, :]
    tail = xb_scr[ROWS_B:ROWS_B + TAIL_B, :]
    xb_scr[0:TAIL_B, :] = tail
    xcb = xc.astype(BF16)

    neg_c_sp = -LRU_C * _softplus(-lam_ref[...])
    for j in range(E_B // GW_B):
        c0 = j * GW_B
        pre = _dot(xcb[:, c0:c0 + GW_B], wbd_ref[j])
        r = _sigmoid(pre[:, 0:GW_B] + ba_ref[:, c0:c0 + GW_B])
        i = _sigmoid(pre[:, GW_B:2 * GW_B] + bx_ref[:, c0:c0 + GW_B])
        log_a = neg_c_sp[:, c0:c0 + GW_B] * r
        t = jnp.tanh(0.5 * log_a)
        d = 1.0 / (1.0 - t)
        a_scr[:, c0:c0 + GW_B] = (1.0 + t) * d
        hs_scr[:, c0:c0 + GW_B] = (2.0 * d) * jnp.sqrt(-t) * (i * xc[:, c0:c0 + GW_B])

    def step(t, hprev):
        r0 = pl.multiple_of(t * BATCH, BATCH)
        hnew = a_scr[pl.ds(r0, BATCH), :] * hprev + hs_scr[pl.ds(r0, BATCH), :]
        hs_scr[pl.ds(r0, BATCH), :] = hnew
        return hnew

    state_scr[...] = lax.fori_loop(0, T_B, step, state_scr[...], unroll=8)

    y = (hs_scr[...] * gate).astype(BF16)
    out = _dot(y, wo_ref[...])
    for c in range(D_MODEL // LANES):
        out_scr[c] = out[:, c * LANES:(c + 1) * LANES]
    for b in range(BATCH):
        ob = jnp.concatenate([out_scr[c, pl.ds(b, T_B, stride=BATCH), :] for c in range(D_MODEL // LANES)], axis=1)
        o_ref[b] = x_ref[b] + ob


def _block_diag_gates(w_a, w_x):
    def bd(w):
        wg = w.reshape(H_B // HG_B, HG_B, BD_B, BD_B)
        eye = jnp.eye(HG_B, dtype=w.dtype)
        return jnp.einsum('ghij,hk->ghikj', wg, eye).reshape(H_B // HG_B, GW_B, GW_B)
    return jnp.concatenate([bd(w_a), bd(w_x)], axis=-1)


def _rglru_layer(x3, norm, w_in, conv_w, conv_b, w_a, b_a, w_x, b_x, lam, w_out):
    ng = H_B // HG_B
    return pl.pallas_call(
        _rglru_kernel,
        grid=(SEQ // T_B,),
        in_specs=[
            pl.BlockSpec((BATCH, T_B, D_MODEL), lambda i: (0, i, 0)),
            _const_spec((1, D_MODEL)),
            _const_spec((D_MODEL, 2 * E_B)),
            _const_spec((CONV_W, E_B)),
            _const_spec((1, E_B)),
            _const_spec((ng, GW_B, 2 * GW_B)),
            _const_spec((1, E_B)),
            _const_spec((1, E_B)),
            _const_spec((1, E_B)),
            _const_spec((E_B, D_MODEL)),
        ],
        out_specs=pl.BlockSpec((BATCH, T_B, D_MODEL), lambda i: (0, i, 0)),
        out_shape=jax.ShapeDtypeStruct((BATCH, SEQ, D_MODEL), F32),
        scratch_shapes=[
            pltpu.VMEM((D_MODEL // LANES, ROWS_B, LANES), F32),
            pltpu.VMEM((TAIL_B + ROWS_B, E_B), F32),
            pltpu.VMEM((ROWS_B, E_B), F32),
            pltpu.VMEM((ROWS_B, E_B), F32),
            pltpu.VMEM((BATCH, E_B), F32),
            pltpu.VMEM((D_MODEL // LANES, ROWS_B, LANES), F32),
        ],
        compiler_params=pltpu.CompilerParams(dimension_semantics=("arbitrary",), vmem_limit_bytes=VMEM_LIMIT),
        name="rglru_layer",
    )(x3, norm.reshape(1, D_MODEL), w_in.astype(BF16), conv_w, conv_b.reshape(1, E_B),
      _block_diag_gates(w_a, w_x).astype(BF16), b_a.reshape(1, E_B), b_x.reshape(1, E_B), lam.reshape(1, E_B),
      w_out.astype(BF16))


def _fox_proj_kernel(x_ref, nrm_ref, win_ref, wf_ref, bf_ref, qg_ref, kg_ref, seg_ref, segt_ref,
                     q_ref, k_ref, v_ref, sg_ref, cum_ref, carry_scr):
    @pl.when(pl.program_id(1) == 0)
    def _():
        carry_scr[...] = jnp.zeros((1, LANES), F32)

    h = _rms(x_ref[0], nrm_ref[...]).astype(BF16)

    def head_norm(z, gain):
        ss = _dot((z * z).astype(BF16), seg_ref[...])
        r = lax.rsqrt(ss * (1.0 / DH_C) + EPS)
        r_hi = r.astype(BF16)
        r_lo = (r - r_hi.astype(F32)).astype(BF16)
        rexp = _dot(r_hi, segt_ref[...]) + _dot(r_lo, segt_ref[...])
        return z * rexp * gain

    q = head_norm(_dot(h, win_ref[:, 0:E_C]), qg_ref[...])
    q_ref[0] = q.astype(BF16)
    k = head_norm(_dot(h, win_ref[:, E_C:2 * E_C]), kg_ref[...])
    k_ref[0] = k.astype(BF16)
    v_ref[0] = _dot(h, win_ref[:, 2 * E_C:3 * E_C]).astype(BF16)
    sg_ref[0] = _silu(_dot(h, win_ref[:, 3 * E_C:4 * E_C])).astype(BF16)

    z = _dot(h, wf_ref[...]) + bf_ref[...]
    log_f = jnp.minimum(z, 0.0) - jnp.log1p(jnp.exp(-jnp.abs(z)))
    row = lax.broadcasted_iota(jnp.int32, (CHUNK, CHUNK), 0)
    col = lax.broadcasted_iota(jnp.int32, (CHUNK, CHUNK), 1)
    tri = jnp.where(col <= row, 1.0, 0.0).astype(F32)
    carry = carry_scr[...]
    for c in range(TM_C // CHUNK):
        r0 = c * CHUNK
        cum = jnp.dot(tri, log_f[r0:r0 + CHUNK], preferred_element_type=F32,
                      precision=lax.Precision.HIGHEST) + carry
        cum_ref[0, r0:r0 + CHUNK, :] = cum[:, 0:H_C]
        carry = cum[CHUNK - 1:CHUNK, :]
    carry_scr[...] = carry


def _fox_proj(x3, norm, w_in, b_f, q_norm, k_norm):
    w_main = w_in[:, 0:4 * E_C].astype(BF16)
    w_f = jnp.pad(w_in[:, 4 * E_C:], ((0, 0), (0, LANES - H_C))).astype(BF16)
    b_fp = jnp.pad(b_f, (0, LANES - H_C)).reshape(1, LANES)
    qg = jnp.tile(q_norm, H_C).reshape(1, E_C) * (DH_C ** -0.5)
    kg = jnp.tile(k_norm, H_C).reshape(1, E_C)
    head_of = jnp.arange(E_C) // DH_C
    seg = (head_of[:, None] == jnp.arange(LANES)[None, :]).astype(BF16)
    tok = jax.ShapeDtypeStruct((BATCH, SEQ, E_C), BF16)
    blk = pl.BlockSpec((1, TM_C, E_C), lambda b, i: (b, i, 0))
    return pl.pallas_call(
        _fox_proj_kernel,
        grid=(BATCH, SEQ // TM_C),
        in_specs=[
            pl.BlockSpec((1, TM_C, D_MODEL), lambda b, i: (b, i, 0)),
            _const_spec((1, D_MODEL)),
            _const_spec((D_MODEL, 4 * E_C)),
            _const_spec((D_MODEL, LANES)),
            _const_spec((1, LANES)),
            _const_spec((1, E_C)),
            _const_spec((1, E_C)),
            _const_spec((E_C, LANES)),
            _const_spec((LANES, E_C)),
        ],
        out_specs=[blk, blk, blk, blk, pl.BlockSpec((1, TM_C, H_C), lambda b, i: (b, i, 0))],
        out_shape=[tok, tok, tok, tok, jax.ShapeDtypeStruct((BATCH, SEQ, H_C), F32)],
        scratch_shapes=[pltpu.VMEM((1, LANES), F32)],
        compiler_params=pltpu.CompilerParams(dimension_semantics=("arbitrary", "arbitrary"),
                                             vmem_limit_bytes=VMEM_LIMIT),
        name="fox_proj",
    )(x3, norm.reshape(1, D_MODEL), w_main, w_f, b_fp, qg, kg, seg, seg.T)


def _attn_kernel(q_ref, k_ref, v_ref, sg_ref, cq_ref, ck_ref, o_ref):
    hp = pl.program_id(1)
    qi = pl.program_id(2)
    q = q_ref[0]
    lane = lax.broadcasted_iota(jnp.int32, (TQ_C, LANES), 1)
    first = lane < DH_C
    cq_all = cq_ref[0]
    head_lane = lax.broadcasted_iota(jnp.int32, (TQ_C, H_C), 1)
    row = lax.broadcasted_iota(jnp.int32, (TQ_C, TK_C), 0)
    col = lax.broadcasted_iota(jnp.int32, (TQ_C, TK_C), 1)
    outs = []
    for hh in range(2):
        qm = jnp.where(first if hh == 0 else jnp.logical_not(first), q, jnp.zeros_like(q))
        cq = jnp.sum(jnp.where(head_lane == 2 * hp + hh, cq_all, 0.0), axis=-1, keepdims=True)

        def scores(j):
            k0 = pl.multiple_of(j * TK_C, TK_C)
            kb = k_ref[0, pl.ds(k0, TK_C), :]
            ck = ck_ref[0, 0, hh:hh + 1, pl.ds(k0, TK_C)]
            s = lax.dot_general(qm, kb, (((1,), (1,)), ((), ())), preferred_element_type=F32)
            return s + (cq - ck), k0

        def update(carry, s, k0):
            m, l, acc = carry
            m_new = jnp.maximum(m, jnp.max(s, axis=-1, keepdims=True))
            alpha = jnp.exp(m - m_new)
            p = jnp.exp(s - m_new)
            l = alpha * l + jnp.sum(p, axis=-1, keepdims=True)
            acc = alpha * acc + _dot(p.astype(BF16), v_ref[0, pl.ds(k0, TK_C), :])
            return m_new, l, acc

        def body(j, carry):
            s, k0 = scores(j)
            return update(carry, s, k0)

        init = (jnp.full((TQ_C, 1), NEG_BIG, F32), jnp.zeros((TQ_C, 1), F32), jnp.zeros((TQ_C, LANES), F32))
        carry = lax.fori_loop(0, qi * (TQ_C // TK_C), body, init)
        s, k0 = scores(qi)
        s = jnp.where(col <= row, s, NEG_BIG)
        _, l, acc = update(carry, s, k0)
        outs.append(acc / l)
    o = jnp.where(first, outs[0], outs[1])
    o_ref[0] = (o * sg_ref[0].astype(F32)).astype(BF16)


def _fox_attention(q, k, v, sg, cum):
    assert TQ_C == TK_C
    ck = cum.transpose(0, 2, 1).reshape(BATCH, H_C // 2, 2, SEQ)
    qblk = pl.BlockSpec((1, TQ_C, LANES), lambda b, hp, i: (b, i, hp))
    kvblk = pl.BlockSpec((1, SEQ, LANES), lambda b, hp, i: (b, 0, hp))
    return pl.pallas_call(
        _attn_kernel,
        grid=(BATCH, H_C // 2, SEQ // TQ_C),
        in_specs=[
            qblk, kvblk, kvblk, qblk,
            pl.BlockSpec((1, TQ_C, H_C), lambda b, hp, i: (b, i, 0)),
            pl.BlockSpec((1, 1, 2, SEQ), lambda b, hp, i: (b, hp, 0, 0)),
        ],
        out_specs=qblk,
        out_shape=jax.ShapeDtypeStruct((BATCH, SEQ, E_C), BF16),
        compiler_params=pltpu.CompilerParams(dimension_semantics=("arbitrary", "arbitrary", "arbitrary"),
                                             vmem_limit_bytes=VMEM_LIMIT),
        name="fox_attention",
    )(q, k, v, sg, cum, ck)


def _out_proj_kernel(y_ref, x_ref, wo_ref, o_ref):
    o_ref[...] = x_ref[...] + _dot(y_ref[...], wo_ref[...])


def _fox_out_proj(y2, x2, w_out):
    n = x2.shape[0]
    return pl.pallas_call(
        _out_proj_kernel,
        grid=(n // TM_C,),
        in_specs=[
            pl.BlockSpec((TM_C, E_C), lambda i: (i, 0)),
            pl.BlockSpec((TM_C, D_MODEL), lambda i: (i, 0)),
            _const_spec((E_C, D_MODEL)),
        ],
        out_specs=pl.BlockSpec((TM_C, D_MODEL), lambda i: (i, 0)),
        out_shape=jax.ShapeDtypeStruct((n, D_MODEL), F32),
        compiler_params=pltpu.CompilerParams(dimension_semantics=("arbitrary",), vmem_limit_bytes=VMEM_LIMIT),
        name="fox_out_proj",
    )(y2, x2, w_out.astype(BF16))


def _fox_layer(x3, norm, w_in, b_f, q_norm, k_norm, w_out):
    q, k, v, sg, cum = _fox_proj(x3, norm, w_in, b_f, q_norm, k_norm)
    y = _fox_attention(q, k, v, sg, cum)
    n = BATCH * SEQ
    return _fox_out_proj(y.reshape(n, E_C), x3.reshape(n, D_MODEL), w_out).reshape(BATCH, SEQ, D_MODEL)


def kernel(x, l0_norm, l0_w_in, l0_v_norm, l0_w_s, l0_b_s, l0_w_out, l1_norm, l1_w_in, l1_conv_w, l1_conv_b, l1_w_a, l1_b_a, l1_w_x, l1_b_x, l1_lam, l1_w_out, l2_norm, l2_w_in, l2_b_f, l2_q_norm, l2_k_norm, l2_w_out, l3_norm, l3_w_in, l3_v_norm, l3_w_s, l3_b_s, l3_w_out):
    n = BATCH * SEQ
    x = _gmlp_layer(x.reshape(n, D_MODEL), l0_norm, l0_w_in, l0_v_norm, l0_w_s, l0_b_s, l0_w_out)
    x = _rglru_layer(x.reshape(BATCH, SEQ, D_MODEL), l1_norm, l1_w_in, l1_conv_w, l1_conv_b, l1_w_a, l1_b_a,
                     l1_w_x, l1_b_x, l1_lam, l1_w_out)
    x = _fox_layer(x, l2_norm, l2_w_in, l2_b_f, l2_q_norm, l2_k_norm, l2_w_out)
    x = _gmlp_layer(x.reshape(n, D_MODEL), l3_norm, l3_w_in, l3_v_norm, l3_w_s, l3_b_s, l3_w_out)
    return x.reshape(BATCH, SEQ, D_MODEL)
```

```python
import jax
import jax.numpy as jnp
from jax import lax
from jax.experimental import pallas as pl
from jax.experimental.pallas import tpu as pltpu

D_MODEL = 1024
BATCH = 8
SEQ = 2048
EPS = 1e-6
CHUNK = 128
E_A = 2 * D_MODEL
G_A = 8
DG_A = E_A // G_A
E_B = 3 * D_MODEL // 2
H_B = 16
BD_B = E_B // H_B
CONV_W = 4
LRU_C = 8.0
H_C = 16
DH_C = D_MODEL // H_C
E_C = H_C * DH_C

LANES = 128
VMEM_LIMIT = 56 * 1024 * 1024

TM_A = 512
T_B = 64
HG_B = 4
GW_B = HG_B * BD_B
TM_C = 512
TQ_C = 256
TK_C = 256
NH_C = 4
NEG_BIG = -1e30
LOG2E = 1.4426950408889634

F32 = jnp.float32
BF16 = jnp.bfloat16


def _dot(a, b):
    return jnp.dot(a, b, preferred_element_type=F32)


def _rms(x, g):
    ms = jnp.mean(x * x, axis=-1, keepdims=True)
    return x * lax.rsqrt(ms + EPS) * g


def _gelu(x):
    return x * (0.5 * (1.0 + jnp.tanh(0.7978845608028654 * (x + 0.044715 * (x * x * x)))))


def _sigmoid(x):
    return 0.5 * (1.0 + jnp.tanh(0.5 * x))


def _silu(x):
    return x * _sigmoid(x)


def _softplus(x):
    return jnp.maximum(x, 0.0) + jnp.log1p(jnp.exp(-jnp.abs(x)))


def _const_spec(shape):
    n = len(shape)
    return pl.BlockSpec(shape, lambda *_: (0,) * n, pipeline_mode=pl.Buffered(1))


def _gmlp_kernel(x_ref, nrm_ref, win_ref, vnrm_ref, ws_ref, bst_ref, wo_ref, o_ref, vn_scr, y_scr):
    x = x_ref[...]
    h = _rms(x, nrm_ref[...]).astype(BF16)
    v = _gelu(_dot(h, win_ref[:, E_A:2 * E_A]))
    vn_scr[...] = _rms(v, vnrm_ref[...]).astype(BF16)
    row = lax.broadcasted_iota(jnp.int32, (CHUNK, CHUNK), 0)
    col = lax.broadcasted_iota(jnp.int32, (CHUNK, CHUNK), 1)
    causal = col <= row
    for g in range(G_A):
        c0 = g * DG_A
        u = _gelu(_dot(h, win_ref[:, c0:c0 + DG_A]))
        gate = _silu(_dot(h, win_ref[:, 2 * E_A + c0:2 * E_A + c0 + DG_A]))
        w = jnp.where(causal, ws_ref[g], 0.0).astype(BF16)
        bias = bst_ref[:, g:g + 1]
        for c in range(TM_A // CHUNK):
            r0 = c * CHUNK
            mixed = _dot(w, vn_scr[r0:r0 + CHUNK, c0:c0 + DG_A]) + bias
            y = u[r0:r0 + CHUNK] * mixed * gate[r0:r0 + CHUNK]
            y_scr[r0:r0 + CHUNK, c0:c0 + DG_A] = y.astype(BF16)
    o_ref[...] = x + _dot(y_scr[...], wo_ref[...])


def _gmlp_layer(x2, norm, w_in, v_norm, w_s, b_s, w_out):
    n = x2.shape[0]
    return pl.pallas_call(
        _gmlp_kernel,
        grid=(n // TM_A,),
        in_specs=[
            pl.BlockSpec((TM_A, D_MODEL), lambda i: (i, 0)),
            _const_spec((1, D_MODEL)),
            _const_spec((D_MODEL, 3 * E_A)),
            _const_spec((1, E_A)),
            _const_spec((G_A, CHUNK, CHUNK)),
            _const_spec((CHUNK, G_A)),
            _const_spec((E_A, D_MODEL)),
        ],
        out_specs=pl.BlockSpec((TM_A, D_MODEL), lambda i: (i, 0)),
        out_shape=jax.ShapeDtypeStruct((n, D_MODEL), F32),
        scratch_shapes=[pltpu.VMEM((TM_A, E_A), BF16), pltpu.VMEM((TM_A, E_A), BF16)],
        compiler_params=pltpu.CompilerParams(dimension_semantics=("arbitrary",), vmem_limit_bytes=VMEM_LIMIT),
        name="gmlp_layer",
    )(x2, norm.reshape(1, D_MODEL), w_in.astype(BF16), v_norm.reshape(1, E_A), w_s, b_s.T, w_out.astype(BF16))


ROWS_B = T_B * BATCH
TAIL_B = (CONV_W - 1) * BATCH


def _rglru_kernel(x_ref, nrm_ref, win_ref, cw_ref, cb_ref, wbd_ref, ba_ref, bx_ref, lam_ref, wo_ref, o_ref,
                  hn_scr, xb_scr, a_scr, hs_scr, state_scr, out_scr):
    @pl.when(pl.program_id(0) == 0)
    def _():
        xb_scr[0:TAIL_B, :] = jnp.zeros((TAIL_B, E_B), F32)
        state_scr[...] = jnp.zeros((BATCH, E_B), F32)

    nrm = nrm_ref[...]
    for b in range(BATCH):
        hb = _rms(x_ref[b], nrm)
        for c in range(D_MODEL // LANES):
            hn_scr[c, pl.ds(b, T_B, stride=BATCH), :] = hb[:, c * LANES:(c + 1) * LANES]
    h = jnp.concatenate([hn_scr[c] for c in range(D_MODEL // LANES)], axis=1).astype(BF16)
    xb_scr[TAIL_B:TAIL_B + ROWS_B, :] = _dot(h, win_ref[:, 0:E_B])
    gate = _silu(_dot(h, win_ref[:, E_B:2 * E_B]))

    xc = cb_ref[...] + cw_ref[0:1, :] * xb_scr[0:ROWS_B, :]
    for k in range(1, CONV_W):
        xc = xc + cw_ref[k:k + 1, :] * xb_scr[k * BATCH:k * BATCH + ROWS_B, :]
    tail = xb_scr[ROWS_B:ROWS_B + TAIL_B, :]
    xb_scr[0:TAIL_B, :] = tail
    xcb = xc.astype(BF16)

    neg_c_sp = -LRU_C * _softplus(-lam_ref[...])
    for j in range(E_B // GW_B):
        c0 = j * GW_B
        pre = _dot(xcb[:, c0:c0 + GW_B], wbd_ref[j])
        r = _sigmoid(pre[:, 0:GW_B] + ba_ref[:, c0:c0 + GW_B])
        i = _sigmoid(pre[:, GW_B:2 * GW_B] + bx_ref[:, c0:c0 + GW_B])
        log_a = neg_c_sp[:, c0:c0 + GW_B] * r
        t = jnp.tanh(0.5 * log_a)
        d = 1.0 / (1.0 - t)
        a_scr[:, c0:c0 + GW_B] = (1.0 + t) * d
        hs_scr[:, c0:c0 + GW_B] = (2.0 * d) * jnp.sqrt(-t) * (i * xc[:, c0:c0 + GW_B])

    def step(t, hprev):
        r0 = pl.multiple_of(t * BATCH, BATCH)
        hnew = a_scr[pl.ds(r0, BATCH), :] * hprev + hs_scr[pl.ds(r0, BATCH), :]
        hs_scr[pl.ds(r0, BATCH), :] = hnew
        return hnew

    state_scr[...] = lax.fori_loop(0, T_B, step, state_scr[...], unroll=8)

    y = (hs_scr[...] * gate).astype(BF16)
    out = _dot(y, wo_ref[...])
    for c in range(D_MODEL // LANES):
        out_scr[c] = out[:, c * LANES:(c + 1) * LANES]
    for b in range(BATCH):
        ob = jnp.concatenate([out_scr[c, pl.ds(b, T_B, stride=BATCH), :] for c in range(D_MODEL // LANES)], axis=1)
        o_ref[b] = x_ref[b] + ob


def _block_diag_gates(w_a, w_x):
    def bd(w):
        wg = w.reshape(H_B // HG_B, HG_B, BD_B, BD_B)
        eye = jnp.eye(HG_B, dtype=w.dtype)
        return jnp.einsum('ghij,hk->ghikj', wg, eye).reshape(H_B // HG_B, GW_B, GW_B)
    return jnp.concatenate([bd(w_a), bd(w_x)], axis=-1)


def _rglru_layer(x3, norm, w_in, conv_w, conv_b, w_a, b_a, w_x, b_x, lam, w_out):
    ng = H_B // HG_B
    return pl.pallas_call(
        _rglru_kernel,
        grid=(SEQ // T_B,),
        in_specs=[
            pl.BlockSpec((BATCH, T_B, D_MODEL), lambda i: (0, i, 0)),
            _const_spec((1, D_MODEL)),
            _const_spec((D_MODEL, 2 * E_B)),
            _const_spec((CONV_W, E_B)),
            _const_spec((1, E_B)),
            _const_spec((ng, GW_B, 2 * GW_B)),
            _const_spec((1, E_B)),
            _const_spec((1, E_B)),
            _const_spec((1, E_B)),
            _const_spec((E_B, D_MODEL)),
        ],
        out_specs=pl.BlockSpec((BATCH, T_B, D_MODEL), lambda i: (0, i, 0)),
        out_shape=jax.ShapeDtypeStruct((BATCH, SEQ, D_MODEL), F32),
        scratch_shapes=[
            pltpu.VMEM((D_MODEL // LANES, ROWS_B, LANES), F32),
            pltpu.VMEM((TAIL_B + ROWS_B, E_B), F32),
            pltpu.VMEM((ROWS_B, E_B), F32),
            pltpu.VMEM((ROWS_B, E_B), F32),
            pltpu.VMEM((BATCH, E_B), F32),
            pltpu.VMEM((D_MODEL // LANES, ROWS_B, LANES), F32),
        ],
        compiler_params=pltpu.CompilerParams(dimension_semantics=("arbitrary",), vmem_limit_bytes=VMEM_LIMIT),
        name="rglru_layer",
    )(x3, norm.reshape(1, D_MODEL), w_in.astype(BF16), conv_w, conv_b.reshape(1, E_B),
      _block_diag_gates(w_a, w_x).astype(BF16), b_a.reshape(1, E_B), b_x.reshape(1, E_B), lam.reshape(1, E_B),
      w_out.astype(BF16))


N_SPLIT = 3


def _fox_proj_kernel(x_ref, nrm_ref, wqkg_ref, wvt_ref, wf_ref, bf_ref, qg_ref, kg_ref, seg_ref, segt_ref, place_ref,
                     q_ref, k_ref, ek_ref, vt_ref, sg_ref, cum_ref, carry_scr):
    @pl.when(pl.program_id(1) == 0)
    def _():
        carry_scr[...] = jnp.zeros((1, LANES), F32)

    h = _rms(x_ref[0], nrm_ref[...]).astype(BF16)

    def head_norm(z, gain):
        ss = _dot((z * z).astype(BF16), seg_ref[...])
        r = lax.rsqrt(ss * (1.0 / DH_C) + EPS)
        r_hi = r.astype(BF16)
        r_lo = (r - r_hi.astype(F32)).astype(BF16)
        rexp = _dot(r_hi, segt_ref[...]) + _dot(r_lo, segt_ref[...])
        return z * rexp * gain

    q = head_norm(_dot(h, wqkg_ref[:, 0:E_C]), qg_ref[...])
    q_ref[0] = q.astype(BF16)
    k = head_norm(_dot(h, wqkg_ref[:, E_C:2 * E_C]), kg_ref[...])
    k_ref[0] = k.astype(BF16)
    sg_ref[0] = _silu(_dot(h, wqkg_ref[:, 2 * E_C:3 * E_C])).astype(BF16)
    vt_ref[0] = lax.dot_general(wvt_ref[...], h, (((1,), (1,)), ((), ())), preferred_element_type=F32).astype(BF16)

    z = _dot(h, wf_ref[...]) + bf_ref[...]
    log_f = jnp.minimum(z, 0.0) - jnp.log1p(jnp.exp(-jnp.abs(z)))
    row = lax.broadcasted_iota(jnp.int32, (CHUNK, CHUNK), 0)
    col = lax.broadcasted_iota(jnp.int32, (CHUNK, CHUNK), 1)
    tri = jnp.where(col <= row, 1.0, 0.0).astype(F32)
    group = lax.broadcasted_iota(jnp.int32, (CHUNK, LANES), 1) // H_C
    carry = carry_scr[...]
    for c in range(TM_C // CHUNK):
        r0 = c * CHUNK
        cum = jnp.dot(tri, log_f[r0:r0 + CHUNK], preferred_element_type=F32,
                      precision=lax.Precision.HIGHEST) + carry
        carry = cum[CHUNK - 1:CHUNK, :]
        c2 = cum * LOG2E
        cum_ref[0, r0:r0 + CHUNK, :] = c2[:, 0:H_C]
        hi = c2.astype(BF16)
        r1 = c2 - hi.astype(F32)
        mid = r1.astype(BF16)
        lo = (r1 - mid.astype(F32)).astype(BF16)
        parts = jnp.where(group == 0, hi, jnp.where(group == 1, mid, lo))
        ek_ref[0, r0:r0 + CHUNK, :] = _dot(parts, place_ref[...]).astype(BF16)
    carry_scr[...] = carry


def _fox_proj(x3, norm, w_in, b_f, q_norm, k_norm):
    w_qkg = jnp.concatenate([w_in[:, 0:2 * E_C], w_in[:, 3 * E_C:4 * E_C]], axis=1).astype(BF16)
    w_vt = w_in[:, 2 * E_C:3 * E_C].T.astype(BF16)
    pad = LANES - N_SPLIT * H_C
    w_f = jnp.pad(jnp.tile(w_in[:, 4 * E_C:], (1, N_SPLIT)), ((0, 0), (0, pad))).astype(BF16)
    b_fp = jnp.pad(jnp.tile(b_f, N_SPLIT), (0, pad)).reshape(1, LANES)
    qg = jnp.tile(q_norm, H_C).reshape(1, E_C) * (DH_C ** -0.5 * LOG2E)
    kg = jnp.tile(k_norm, H_C).reshape(1, E_C)
    head_of = jnp.arange(E_C) // DH_C
    seg = (head_of[:, None] == jnp.arange(LANES)[None, :]).astype(BF16)
    src = jnp.arange(LANES)
    dst = (src % H_C) * DH_C + src // H_C
    place = jnp.where((src[:, None] < N_SPLIT * H_C) & (dst[:, None] == jnp.arange(E_C)[None, :]), -1.0, 0.0)
    tok = jax.ShapeDtypeStruct((BATCH, SEQ, E_C), BF16)
    blk = pl.BlockSpec((1, TM_C, E_C), lambda b, i: (b, i, 0))
    return pl.pallas_call(
        _fox_proj_kernel,
        grid=(BATCH, SEQ // TM_C),
        in_specs=[
            pl.BlockSpec((1, TM_C, D_MODEL), lambda b, i: (b, i, 0)),
            _const_spec((1, D_MODEL)),
            _const_spec((D_MODEL, 3 * E_C)),
            _const_spec((E_C, D_MODEL)),
            _const_spec((D_MODEL, LANES)),
            _const_spec((1, LANES)),
            _const_spec((1, E_C)),
            _const_spec((1, E_C)),
            _const_spec((E_C, LANES)),
            _const_spec((LANES, E_C)),
            _const_spec((LANES, E_C)),
        ],
        out_specs=[blk, blk, blk, pl.BlockSpec((1, E_C, TM_C), lambda b, i: (b, 0, i)), blk,
                   pl.BlockSpec((1, TM_C, H_C), lambda b, i: (b, i, 0))],
        out_shape=[tok, tok, tok, jax.ShapeDtypeStruct((BATCH, E_C, SEQ), BF16), tok,
                   jax.ShapeDtypeStruct((BATCH, SEQ, H_C), F32)],
        scratch_shapes=[pltpu.VMEM((1, LANES), F32)],
        compiler_params=pltpu.CompilerParams(dimension_semantics=("arbitrary", "arbitrary"),
                                             vmem_limit_bytes=VMEM_LIMIT),
        name="fox_proj",
    )(x3, norm.reshape(1, D_MODEL), w_qkg, w_vt, w_f, b_fp, qg, kg, seg, seg.T, place.astype(BF16))


def _attn_kernel(q_ref, k_ref, ek_ref, vt_ref, sg_ref, cq_ref, o_ref, qa_scr, acc_scr):
    qi = pl.program_id(2)
    lane_q = lax.broadcasted_iota(jnp.int32, (TQ_C, LANES), 1)
    first_q = lane_q < DH_C
    extra_q = lane_q % DH_C < N_SPLIT
    first_v = lax.broadcasted_iota(jnp.int32, (LANES, TK_C), 0) < DH_C
    key = lax.broadcasted_iota(jnp.int32, (TK_C, TQ_C), 0)
    qry = lax.broadcasted_iota(jnp.int32, (TK_C, TQ_C), 1)
    ones_v = jnp.ones((LANES, TK_C), BF16)
    heads = range(NH_C)
    cq = []
    for hh in heads:
        pair, odd = divmod(hh, 2)
        qf = q_ref[0, :, pair * LANES:(pair + 1) * LANES].astype(F32)
        mine = jnp.logical_not(first_q) if odd else first_q
        qa_scr[hh, :, 0:LANES] = jnp.where(mine, qf, 0.0).astype(BF16)
        qa_scr[hh, :, LANES:2 * LANES] = jnp.where(jnp.logical_and(mine, extra_q), 1.0, 0.0).astype(BF16)
        cq.append(cq_ref[0, 0, hh:hh + 1, :])
    acc_scr[...] = jnp.zeros(acc_scr.shape, F32)

    def tile(j, m_old, masked):
        k0 = pl.multiple_of(j * TK_C, TK_C)
        st = []
        for hh in heads:
            pair = hh // 2
            ka = jnp.concatenate([k_ref[0, pl.ds(k0, TK_C), pair * LANES:(pair + 1) * LANES],
                                  ek_ref[0, pl.ds(k0, TK_C), pair * LANES:(pair + 1) * LANES]], axis=1)
            s = lax.dot_general(ka, qa_scr[hh], (((1,), (1,)), ((), ())), preferred_element_type=F32)
            st.append(jnp.where(key <= qry, s, NEG_BIG) if masked else s)
        m_new, pt = [], []
        for hh in heads:
            m = jnp.maximum(m_old[hh], jnp.max(st[hh], axis=0, keepdims=True) + cq[hh])
            pt.append(jnp.exp2(st[hh] - (m - cq[hh])).astype(BF16))
            m_new.append(m)
        pv = []
        for hh in heads:
            pair, odd = divmod(hh, 2)
            vt = vt_ref[0, pair * LANES:(pair + 1) * LANES, pl.ds(k0, TK_C)]
            va = jnp.where(first_v, ones_v, vt) if odd else jnp.where(first_v, vt, ones_v)
            pv.append(_dot(va, pt[hh]))
        for hh in heads:
            acc_scr[hh] = jnp.exp2(m_old[hh] - m_new[hh]) * acc_scr[hh] + pv[hh]
        return tuple(m_new)

    m_init = (jnp.full((1, TQ_C), NEG_BIG, F32),) * NH_C
    m_run = lax.fori_loop(0, qi, lambda j, m: tile(j, m, False), m_init)
    tile(qi, m_run, True)

    for pair in range(NH_C // 2):
        a0 = acc_scr[2 * pair]
        a1 = acc_scr[2 * pair + 1]
        ot = jnp.concatenate([a0[0:DH_C] / a0[DH_C:2 * DH_C], a1[DH_C:2 * DH_C] / a1[0:DH_C]], axis=0)
        cols = slice(pair * LANES, (pair + 1) * LANES)
        o_ref[0, :, cols] = (ot.T * sg_ref[0, :, cols].astype(F32)).astype(BF16)


def _fox_attention(q, k, ek, vt, sg, cum):
    assert TQ_C == TK_C
    ng = H_C // NH_C
    w = NH_C * DH_C
    cq = cum.transpose(0, 2, 1).reshape(BATCH, ng, NH_C, SEQ)
    qblk = pl.BlockSpec((1, TQ_C, w), lambda b, g, i: (b, i, g))
    kblk = pl.BlockSpec((1, SEQ, w), lambda b, g, i: (b, 0, g))
    return pl.pallas_call(
        _attn_kernel,
        grid=(BATCH, ng, SEQ // TQ_C),
        in_specs=[
            qblk, kblk, kblk,
            pl.BlockSpec((1, w, SEQ), lambda b, g, i: (b, g, 0)),
            qblk,
            pl.BlockSpec((1, 1, NH_C, TQ_C), lambda b, g, i: (b, g, 0, i)),
        ],
        out_specs=qblk,
        out_shape=jax.ShapeDtypeStruct((BATCH, SEQ, E_C), BF16),
        scratch_shapes=[pltpu.VMEM((NH_C, TQ_C, 2 * LANES), BF16), pltpu.VMEM((NH_C, LANES, TQ_C), F32)],
        compiler_params=pltpu.CompilerParams(dimension_semantics=("arbitrary", "arbitrary", "arbitrary"),
                                             vmem_limit_bytes=VMEM_LIMIT),
        name="fox_attention",
    )(q, k, ek, vt, sg, cq)


def _out_proj_kernel(y_ref, x_ref, wo_ref, o_ref):
    o_ref[...] = x_ref[...] + _dot(y_ref[...], wo_ref[...])


def _fox_out_proj(y2, x2, w_out):
    n = x2.shape[0]
    return pl.pallas_call(
        _out_proj_kernel,
        grid=(n // TM_C,),
        in_specs=[
            pl.BlockSpec((TM_C, E_C), lambda i: (i, 0)),
            pl.BlockSpec((TM_C, D_MODEL), lambda i: (i, 0)),
            _const_spec((E_C, D_MODEL)),
        ],
        out_specs=pl.BlockSpec((TM_C, D_MODEL), lambda i: (i, 0)),
        out_shape=jax.ShapeDtypeStruct((n, D_MODEL), F32),
        compiler_params=pltpu.CompilerParams(dimension_semantics=("arbitrary",), vmem_limit_bytes=VMEM_LIMIT),
        name="fox_out_proj",
    )(y2, x2, w_out.astype(BF16))


def _fox_layer(x3, norm, w_in, b_f, q_norm, k_norm, w_out):
    q, k, ek, vt, sg, cum = _fox_proj(x3, norm, w_in, b_f, q_norm, k_norm)
    y = _fox_attention(q, k, ek, vt, sg, cum)
    n = BATCH * SEQ
    return _fox_out_proj(y.reshape(n, E_C), x3.reshape(n, D_MODEL), w_out).reshape(BATCH, SEQ, D_MODEL)


def kernel(x, l0_norm, l0_w_in, l0_v_norm, l0_w_s, l0_b_s, l0_w_out, l1_norm, l1_w_in, l1_conv_w, l1_conv_b, l1_w_a, l1_b_a, l1_w_x, l1_b_x, l1_lam, l1_w_out, l2_norm, l2_w_in, l2_b_f, l2_q_norm, l2_k_norm, l2_w_out, l3_norm, l3_w_in, l3_v_norm, l3_w_s, l3_b_s, l3_w_out):
    n = BATCH * SEQ
    x = _gmlp_layer(x.reshape(n, D_MODEL), l0_norm, l0_w_in, l0_v_norm, l0_w_s, l0_b_s, l0_w_out)
    x = _rglru_layer(x.reshape(BATCH, SEQ, D_MODEL), l1_norm, l1_w_in, l1_conv_w, l1_conv_b, l1_w_a, l1_b_a,
                     l1_w_x, l1_b_x, l1_lam, l1_w_out)
    x = _fox_layer(x, l2_norm, l2_w_in, l2_b_f, l2_q_norm, l2_k_norm, l2_w_out)
    x = _gmlp_layer(x.reshape(n, D_MODEL), l3_norm, l3_w_in, l3_v_norm, l3_w_s, l3_b_s, l3_w_out)
    return x.reshape(BATCH, SEQ, D_MODEL)
```

```python
import jax
import jax.numpy as jnp
from jax import lax
from jax.experimental import pallas as pl
from jax.experimental.pallas import tpu as pltpu

D_MODEL = 1024
BATCH = 8
SEQ = 2048
EPS = 1e-6
CHUNK = 128
E_A = 2 * D_MODEL
G_A = 8
DG_A = E_A // G_A
E_B = 3 * D_MODEL // 2
H_B = 16
BD_B = E_B // H_B
CONV_W = 4
LRU_C = 8.0
H_C = 16
DH_C = D_MODEL // H_C
E_C = H_C * DH_C

LANES = 128
VMEM_LIMIT = 56 * 1024 * 1024

TM_A = 512
T_B = 64
HG_B = 4
GW_B = HG_B * BD_B
TM_C = 512
TQ_C = 256
TK_C = 256
NH_C = 4
DEN_ROWS = 16
NEG_BIG = -1e30
LOG2E = 1.4426950408889634

F32 = jnp.float32
BF16 = jnp.bfloat16


def _dot(a, b):
    return jnp.dot(a, b, preferred_element_type=F32)


def _rms(x, g):
    ms = jnp.mean(x * x, axis=-1, keepdims=True)
    return x * lax.rsqrt(ms + EPS) * g


def _gelu(x):
    return x * (0.5 * (1.0 + jnp.tanh(0.7978845608028654 * (x + 0.044715 * (x * x * x)))))


def _sigmoid(x):
    return 0.5 * (1.0 + jnp.tanh(0.5 * x))


def _silu(x):
    return x * _sigmoid(x)


def _softplus(x):
    return jnp.maximum(x, 0.0) + jnp.log1p(jnp.exp(-jnp.abs(x)))


def _const_spec(shape):
    n = len(shape)
    return pl.BlockSpec(shape, lambda *_: (0,) * n, pipeline_mode=pl.Buffered(1))


def _gmlp_kernel(x_ref, nrm_ref, win_ref, vnrm_ref, ws_ref, bst_ref, wo_ref, o_ref, vn_scr, y_scr):
    x = x_ref[...]
    h = _rms(x, nrm_ref[...]).astype(BF16)
    v = _gelu(_dot(h, win_ref[:, E_A:2 * E_A]))
    vn_scr[...] = _rms(v, vnrm_ref[...]).astype(BF16)
    row = lax.broadcasted_iota(jnp.int32, (CHUNK, CHUNK), 0)
    col = lax.broadcasted_iota(jnp.int32, (CHUNK, CHUNK), 1)
    causal = col <= row
    for g in range(G_A):
        c0 = g * DG_A
        u = _gelu(_dot(h, win_ref[:, c0:c0 + DG_A]))
        gate = _silu(_dot(h, win_ref[:, 2 * E_A + c0:2 * E_A + c0 + DG_A]))
        w = jnp.where(causal, ws_ref[g], 0.0).astype(BF16)
        bias = bst_ref[:, g:g + 1]
        for c in range(TM_A // CHUNK):
            r0 = c * CHUNK
            mixed = _dot(w, vn_scr[r0:r0 + CHUNK, c0:c0 + DG_A]) + bias
            y = u[r0:r0 + CHUNK] * mixed * gate[r0:r0 + CHUNK]
            y_scr[r0:r0 + CHUNK, c0:c0 + DG_A] = y.astype(BF16)
    o_ref[...] = x + _dot(y_scr[...], wo_ref[...])


def _gmlp_layer(x2, norm, w_in, v_norm, w_s, b_s, w_out):
    n = x2.shape[0]
    return pl.pallas_call(
        _gmlp_kernel,
        grid=(n // TM_A,),
        in_specs=[
            pl.BlockSpec((TM_A, D_MODEL), lambda i: (i, 0)),
            _const_spec((1, D_MODEL)),
            _const_spec((D_MODEL, 3 * E_A)),
            _const_spec((1, E_A)),
            _const_spec((G_A, CHUNK, CHUNK)),
            _const_spec((CHUNK, G_A)),
            _const_spec((E_A, D_MODEL)),
        ],
        out_specs=pl.BlockSpec((TM_A, D_MODEL), lambda i: (i, 0)),
        out_shape=jax.ShapeDtypeStruct((n, D_MODEL), F32),
        scratch_shapes=[pltpu.VMEM((TM_A, E_A), BF16), pltpu.VMEM((TM_A, E_A), BF16)],
        compiler_params=pltpu.CompilerParams(dimension_semantics=("arbitrary",), vmem_limit_bytes=VMEM_LIMIT),
        name="gmlp_layer",
    )(x2, norm.reshape(1, D_MODEL), w_in.astype(BF16), v_norm.reshape(1, E_A), w_s, b_s.T, w_out.astype(BF16))


ROWS_B = T_B * BATCH
TAIL_B = (CONV_W - 1) * BATCH


def _rglru_kernel(x_ref, nrm_ref, win_ref, cw_ref, cb_ref, wbd_ref, ba_ref, bx_ref, lam_ref, wo_ref, o_ref,
                  hn_scr, xb_scr, a_scr, hs_scr, state_scr, out_scr):
    @pl.when(pl.program_id(0) == 0)
    def _():
        xb_scr[0:TAIL_B, :] = jnp.zeros((TAIL_B, E_B), F32)
        state_scr[...] = jnp.zeros((BATCH, E_B), F32)

    nrm = nrm_ref[...]
    for b in range(BATCH):
        hb = _rms(x_ref[b], nrm)
        for c in range(D_MODEL // LANES):
            hn_scr[c, pl.ds(b, T_B, stride=BATCH), :] = hb[:, c * LANES:(c + 1) * LANES]
    h = jnp.concatenate([hn_scr[c] for c in range(D_MODEL // LANES)], axis=1).astype(BF16)
    xb_scr[TAIL_B:TAIL_B + ROWS_B, :] = _dot(h, win_ref[:, 0:E_B])
    gate = _silu(_dot(h, win_ref[:, E_B:2 * E_B]))

    xc = cb_ref[...] + cw_ref[0:1, :] * xb_scr[0:ROWS_B, :]
    for k in range(1, CONV_W):
        xc = xc + cw_ref[k:k + 1, :] * xb_scr[k * BATCH:k * BATCH + ROWS_B, :]
    tail = xb_scr[ROWS_B:ROWS_B + TAIL_B, :]
    xb_scr[0:TAIL_B, :] = tail
    xcb = xc.astype(BF16)

    neg_c_sp = -LRU_C * _softplus(-lam_ref[...])
    for j in range(E_B // GW_B):
        c0 = j * GW_B
        pre = _dot(xcb[:, c0:c0 + GW_B], wbd_ref[j])
        r = _sigmoid(pre[:, 0:GW_B] + ba_ref[:, c0:c0 + GW_B])
        i = _sigmoid(pre[:, GW_B:2 * GW_B] + bx_ref[:, c0:c0 + GW_B])
        log_a = neg_c_sp[:, c0:c0 + GW_B] * r
        t = jnp.tanh(0.5 * log_a)
        d = 1.0 / (1.0 - t)
        a_scr[:, c0:c0 + GW_B] = (1.0 + t) * d
        hs_scr[:, c0:c0 + GW_B] = (2.0 * d) * jnp.sqrt(-t) * (i * xc[:, c0:c0 + GW_B])

    def step(t, hprev):
        r0 = pl.multiple_of(t * BATCH, BATCH)
        hnew = a_scr[pl.ds(r0, BATCH), :] * hprev + hs_scr[pl.ds(r0, BATCH), :]
        hs_scr[pl.ds(r0, BATCH), :] = hnew
        return hnew

    state_scr[...] = lax.fori_loop(0, T_B, step, state_scr[...], unroll=8)

    y = (hs_scr[...] * gate).astype(BF16)
    out = _dot(y, wo_ref[...])
    for c in range(D_MODEL // LANES):
        out_scr[c] = out[:, c * LANES:(c + 1) * LANES]
    for b in range(BATCH):
        ob = jnp.concatenate([out_scr[c, pl.ds(b, T_B, stride=BATCH), :] for c in range(D_MODEL // LANES)], axis=1)
        o_ref[b] = x_ref[b] + ob


def _block_diag_gates(w_a, w_x):
    def bd(w):
        wg = w.reshape(H_B // HG_B, HG_B, BD_B, BD_B)
        eye = jnp.eye(HG_B, dtype=w.dtype)
        return jnp.einsum('ghij,hk->ghikj', wg, eye).reshape(H_B // HG_B, GW_B, GW_B)
    return jnp.concatenate([bd(w_a), bd(w_x)], axis=-1)


def _rglru_layer(x3, norm, w_in, conv_w, conv_b, w_a, b_a, w_x, b_x, lam, w_out):
    ng = H_B // HG_B
    return pl.pallas_call(
        _rglru_kernel,
        grid=(SEQ // T_B,),
        in_specs=[
            pl.BlockSpec((BATCH, T_B, D_MODEL), lambda i: (0, i, 0)),
            _const_spec((1, D_MODEL)),
            _const_spec((D_MODEL, 2 * E_B)),
            _const_spec((CONV_W, E_B)),
            _const_spec((1, E_B)),
            _const_spec((ng, GW_B, 2 * GW_B)),
            _const_spec((1, E_B)),
            _const_spec((1, E_B)),
            _const_spec((1, E_B)),
            _const_spec((E_B, D_MODEL)),
        ],
        out_specs=pl.BlockSpec((BATCH, T_B, D_MODEL), lambda i: (0, i, 0)),
        out_shape=jax.ShapeDtypeStruct((BATCH, SEQ, D_MODEL), F32),
        scratch_shapes=[
            pltpu.VMEM((D_MODEL // LANES, ROWS_B, LANES), F32),
            pltpu.VMEM((TAIL_B + ROWS_B, E_B), F32),
            pltpu.VMEM((ROWS_B, E_B), F32),
            pltpu.VMEM((ROWS_B, E_B), F32),
            pltpu.VMEM((BATCH, E_B), F32),
            pltpu.VMEM((D_MODEL // LANES, ROWS_B, LANES), F32),
        ],
        compiler_params=pltpu.CompilerParams(dimension_semantics=("arbitrary",), vmem_limit_bytes=VMEM_LIMIT),
        name="rglru_layer",
    )(x3, norm.reshape(1, D_MODEL), w_in.astype(BF16), conv_w, conv_b.reshape(1, E_B),
      _block_diag_gates(w_a, w_x).astype(BF16), b_a.reshape(1, E_B), b_x.reshape(1, E_B), lam.reshape(1, E_B),
      w_out.astype(BF16))


N_SPLIT = 3


def _fox_proj_kernel(x_ref, nrm_ref, wqkg_ref, wvt_ref, wf_ref, bf_ref, qg_ref, kg_ref, seg_ref, segt_ref, place_ref,
                     q_ref, k_ref, ek_ref, vt_ref, sg_ref, cum_ref, carry_scr):
    @pl.when(pl.program_id(1) == 0)
    def _():
        carry_scr[...] = jnp.zeros((1, LANES), F32)

    h = _rms(x_ref[0], nrm_ref[...]).astype(BF16)

    def head_norm(z, gain):
        ss = _dot((z * z).astype(BF16), seg_ref[...])
        r = lax.rsqrt(ss * (1.0 / DH_C) + EPS)
        r_hi = r.astype(BF16)
        r_lo = (r - r_hi.astype(F32)).astype(BF16)
        rexp = _dot(r_hi, segt_ref[...]) + _dot(r_lo, segt_ref[...])
        return z * rexp * gain

    q = head_norm(_dot(h, wqkg_ref[:, 0:E_C]), qg_ref[...])
    q_ref[0] = q.astype(BF16)
    k = head_norm(_dot(h, wqkg_ref[:, E_C:2 * E_C]), kg_ref[...])
    k_ref[0] = k.astype(BF16)
    sg_ref[0] = _silu(_dot(h, wqkg_ref[:, 2 * E_C:3 * E_C])).astype(BF16)
    vt_ref[0] = lax.dot_general(wvt_ref[...], h, (((1,), (1,)), ((), ())), preferred_element_type=F32).astype(BF16)

    z = _dot(h, wf_ref[...]) + bf_ref[...]
    log_f = jnp.minimum(z, 0.0) - jnp.log1p(jnp.exp(-jnp.abs(z)))
    row = lax.broadcasted_iota(jnp.int32, (CHUNK, CHUNK), 0)
    col = lax.broadcasted_iota(jnp.int32, (CHUNK, CHUNK), 1)
    tri = jnp.where(col <= row, 1.0, 0.0).astype(F32)
    group = lax.broadcasted_iota(jnp.int32, (CHUNK, LANES), 1) // H_C
    carry = carry_scr[...]
    for c in range(TM_C // CHUNK):
        r0 = c * CHUNK
        cum = jnp.dot(tri, log_f[r0:r0 + CHUNK], preferred_element_type=F32,
                      precision=lax.Precision.HIGHEST) + carry
        carry = cum[CHUNK - 1:CHUNK, :]
        c2 = cum * LOG2E
        cum_ref[0, r0:r0 + CHUNK, :] = c2[:, 0:H_C]
        hi = c2.astype(BF16)
        r1 = c2 - hi.astype(F32)
        mid = r1.astype(BF16)
        lo = (r1 - mid.astype(F32)).astype(BF16)
        parts = jnp.where(group == 0, hi, jnp.where(group == 1, mid, lo))
        ek_ref[0, r0:r0 + CHUNK, :] = _dot(parts, place_ref[...]).astype(BF16)
    carry_scr[...] = carry


def _fox_proj(x3, norm, w_in, b_f, q_norm, k_norm):
    w_qkg = jnp.concatenate([w_in[:, 0:2 * E_C], w_in[:, 3 * E_C:4 * E_C]], axis=1).astype(BF16)
    w_vt = w_in[:, 2 * E_C:3 * E_C].T.astype(BF16)
    pad = LANES - N_SPLIT * H_C
    w_f = jnp.pad(jnp.tile(w_in[:, 4 * E_C:], (1, N_SPLIT)), ((0, 0), (0, pad))).astype(BF16)
    b_fp = jnp.pad(jnp.tile(b_f, N_SPLIT), (0, pad)).reshape(1, LANES)
    qg = jnp.tile(q_norm, H_C).reshape(1, E_C) * (DH_C ** -0.5 * LOG2E)
    kg = jnp.tile(k_norm, H_C).reshape(1, E_C)
    head_of = jnp.arange(E_C) // DH_C
    seg = (head_of[:, None] == jnp.arange(LANES)[None, :]).astype(BF16)
    src = jnp.arange(LANES)
    dst = (src % H_C) * DH_C + src // H_C
    place = jnp.where((src[:, None] < N_SPLIT * H_C) & (dst[:, None] == jnp.arange(E_C)[None, :]), -1.0, 0.0)
    tok = jax.ShapeDtypeStruct((BATCH, SEQ, E_C), BF16)
    blk = pl.BlockSpec((1, TM_C, E_C), lambda b, i: (b, i, 0))
    return pl.pallas_call(
        _fox_proj_kernel,
        grid=(BATCH, SEQ // TM_C),
        in_specs=[
            pl.BlockSpec((1, TM_C, D_MODEL), lambda b, i: (b, i, 0)),
            _const_spec((1, D_MODEL)),
            _const_spec((D_MODEL, 3 * E_C)),
            _const_spec((E_C, D_MODEL)),
            _const_spec((D_MODEL, LANES)),
            _const_spec((1, LANES)),
            _const_spec((1, E_C)),
            _const_spec((1, E_C)),
            _const_spec((E_C, LANES)),
            _const_spec((LANES, E_C)),
            _const_spec((LANES, E_C)),
        ],
        out_specs=[blk, blk, blk, pl.BlockSpec((1, E_C, TM_C), lambda b, i: (b, 0, i)), blk,
                   pl.BlockSpec((1, TM_C, H_C), lambda b, i: (b, i, 0))],
        out_shape=[tok, tok, tok, jax.ShapeDtypeStruct((BATCH, E_C, SEQ), BF16), tok,
                   jax.ShapeDtypeStruct((BATCH, SEQ, H_C), F32)],
        scratch_shapes=[pltpu.VMEM((1, LANES), F32)],
        compiler_params=pltpu.CompilerParams(dimension_semantics=("arbitrary", "arbitrary"),
                                             vmem_limit_bytes=VMEM_LIMIT),
        name="fox_proj",
    )(x3, norm.reshape(1, D_MODEL), w_qkg, w_vt, w_f, b_fp, qg, kg, seg, seg.T, place.astype(BF16))


def _attn_kernel(q_ref, k_ref, ek_ref, vt_ref, sg_ref, cq_ref, o_ref, qa_scr, acc_scr, m_scr, sta_scr, stb_scr):
    qi = pl.program_id(2)
    lane_q = lax.broadcasted_iota(jnp.int32, (TQ_C, LANES), 1)
    first_q = lane_q < DH_C
    extra_q = lane_q % DH_C < N_SPLIT
    key = lax.broadcasted_iota(jnp.int32, (TK_C, TQ_C), 0)
    qry = lax.broadcasted_iota(jnp.int32, (TK_C, TQ_C), 1)
    ones_v = jnp.ones((DEN_ROWS, TK_C), BF16)
    heads = range(NH_C)
    cq = []
    for hh in heads:
        pair, odd = divmod(hh, 2)
        qf = q_ref[0, :, pair * LANES:(pair + 1) * LANES].astype(F32)
        mine = jnp.logical_not(first_q) if odd else first_q
        qa_scr[hh, :, 0:LANES] = jnp.where(mine, qf, 0.0).astype(BF16)
        qa_scr[hh, :, LANES:2 * LANES] = jnp.where(jnp.logical_and(mine, extra_q), 1.0, 0.0).astype(BF16)
        cq.append(cq_ref[0, 0, hh:hh + 1, :])
    acc_scr[...] = jnp.zeros(acc_scr.shape, F32)

    m_scr[...] = jnp.full(m_scr.shape, NEG_BIG, F32)

    def scores(j, st_ref):
        k0 = pl.multiple_of(j * TK_C, TK_C)
        for hh in heads:
            pair = hh // 2
            ka = jnp.concatenate([k_ref[0, pl.ds(k0, TK_C), pair * LANES:(pair + 1) * LANES],
                                  ek_ref[0, pl.ds(k0, TK_C), pair * LANES:(pair + 1) * LANES]], axis=1)
            st_ref[hh] = lax.dot_general(ka, qa_scr[hh], (((1,), (1,)), ((), ())),
                                         preferred_element_type=F32)

    def consume(j, st_ref, masked):
        k0 = pl.multiple_of(j * TK_C, TK_C)
        alpha, pt = [], []
        for hh in heads:
            st = st_ref[hh]
            if masked:
                st = jnp.where(key <= qry, st, NEG_BIG)
            m_old = m_scr[hh]
            m = jnp.maximum(m_old, jnp.max(st, axis=0, keepdims=True) + cq[hh])
            pt.append(jnp.exp2(st - (m - cq[hh])).astype(BF16))
            alpha.append(jnp.exp2(m_old - m))
            m_scr[hh] = m
        pv = []
        for hh in heads:
            vt = vt_ref[0, hh * DH_C:(hh + 1) * DH_C, pl.ds(k0, TK_C)]
            pv.append(_dot(jnp.concatenate([vt, ones_v], axis=0), pt[hh]))
        for hh in heads:
            acc_scr[hh] = alpha[hh] * acc_scr[hh] + pv[hh]

    def step(i, carry):
        j = 2 * i
        scores(j + 1, stb_scr)
        consume(j, sta_scr, False)
        scores(j + 2, sta_scr)
        consume(j + 1, stb_scr, False)
        return carry

    scores(0, sta_scr)
    lax.fori_loop(0, qi // 2, step, 0)

    @pl.when(qi % 2 == 0)
    def _():
        consume(qi, sta_scr, True)

    @pl.when(qi % 2 == 1)
    def _():
        scores(qi, stb_scr)
        consume(qi - 1, sta_scr, False)
        consume(qi, stb_scr, True)

    for pair in range(NH_C // 2):
        a0 = acc_scr[2 * pair]
        a1 = acc_scr[2 * pair + 1]
        ot = jnp.concatenate([a0[0:DH_C] / a0[DH_C:DH_C + 1], a1[0:DH_C] / a1[DH_C:DH_C + 1]], axis=0)
        cols = slice(pair * LANES, (pair + 1) * LANES)
        o_ref[0, :, cols] = (ot.T * sg_ref[0, :, cols].astype(F32)).astype(BF16)


def _fox_attention(q, k, ek, vt, sg, cum):
    assert TQ_C == TK_C
    ng = H_C // NH_C
    w = NH_C * DH_C
    cq = cum.transpose(0, 2, 1).reshape(BATCH, ng, NH_C, SEQ)
    qblk = pl.BlockSpec((1, TQ_C, w), lambda b, g, i: (b, i, g))
    kblk = pl.BlockSpec((1, SEQ, w), lambda b, g, i: (b, 0, g))
    return pl.pallas_call(
        _attn_kernel,
        grid=(BATCH, ng, SEQ // TQ_C),
        in_specs=[
            qblk, kblk, kblk,
            pl.BlockSpec((1, w, SEQ), lambda b, g, i: (b, g, 0)),
            qblk,
            pl.BlockSpec((1, 1, NH_C, TQ_C), lambda b, g, i: (b, g, 0, i)),
        ],
        out_specs=qblk,
        out_shape=jax.ShapeDtypeStruct((BATCH, SEQ, E_C), BF16),
        scratch_shapes=[pltpu.VMEM((NH_C, TQ_C, 2 * LANES), BF16), pltpu.VMEM((NH_C, DH_C + DEN_ROWS, TQ_C), F32),
                        pltpu.VMEM((NH_C, 1, TQ_C), F32),
                        pltpu.VMEM((NH_C, TK_C, TQ_C), F32), pltpu.VMEM((NH_C, TK_C, TQ_C), F32)],
        compiler_params=pltpu.CompilerParams(dimension_semantics=("arbitrary", "arbitrary", "arbitrary"),
                                             vmem_limit_bytes=VMEM_LIMIT),
        name="fox_attention",
    )(q, k, ek, vt, sg, cq)


def _out_proj_kernel(y_ref, x_ref, wo_ref, o_ref):
    o_ref[...] = x_ref[...] + _dot(y_ref[...], wo_ref[...])


def _fox_out_proj(y2, x2, w_out):
    n = x2.shape[0]
    return pl.pallas_call(
        _out_proj_kernel,
        grid=(n // TM_C,),
        in_specs=[
            pl.BlockSpec((TM_C, E_C), lambda i: (i, 0)),
            pl.BlockSpec((TM_C, D_MODEL), lambda i: (i, 0)),
            _const_spec((E_C, D_MODEL)),
        ],
        out_specs=pl.BlockSpec((TM_C, D_MODEL), lambda i: (i, 0)),
        out_shape=jax.ShapeDtypeStruct((n, D_MODEL), F32),
        compiler_params=pltpu.CompilerParams(dimension_semantics=("arbitrary",), vmem_limit_bytes=VMEM_LIMIT),
        name="fox_out_proj",
    )(y2, x2, w_out.astype(BF16))


def _fox_layer(x3, norm, w_in, b_f, q_norm, k_norm, w_out):
    q, k, ek, vt, sg, cum = _fox_proj(x3, norm, w_in, b_f, q_norm, k_norm)
    y = _fox_attention(q, k, ek, vt, sg, cum)
    n = BATCH * SEQ
    return _fox_out_proj(y.reshape(n, E_C), x3.reshape(n, D_MODEL), w_out).reshape(BATCH, SEQ, D_MODEL)


def kernel(x, l0_norm, l0_w_in, l0_v_norm, l0_w_s, l0_b_s, l0_w_out, l1_norm, l1_w_in, l1_conv_w, l1_conv_b, l1_w_a, l1_b_a, l1_w_x, l1_b_x, l1_lam, l1_w_out, l2_norm, l2_w_in, l2_b_f, l2_q_norm, l2_k_norm, l2_w_out, l3_norm, l3_w_in, l3_v_norm, l3_w_s, l3_b_s, l3_w_out):
    n = BATCH * SEQ
    x = _gmlp_layer(x.reshape(n, D_MODEL), l0_norm, l0_w_in, l0_v_norm, l0_w_s, l0_b_s, l0_w_out)
    x = _rglru_layer(x.reshape(BATCH, SEQ, D_MODEL), l1_norm, l1_w_in, l1_conv_w, l1_conv_b, l1_w_a, l1_b_a,
                     l1_w_x, l1_b_x, l1_lam, l1_w_out)
    x = _fox_layer(x, l2_norm, l2_w_in, l2_b_f, l2_q_norm, l2_k_norm, l2_w_out)
    x = _gmlp_layer(x.reshape(n, D_MODEL), l3_norm, l3_w_in, l3_v_norm, l3_w_s, l3_b_s, l3_w_out)
    return x.reshape(BATCH, SEQ, D_MODEL)
```

```python
import jax
import jax.numpy as jnp
from jax import lax
from jax.experimental import pallas as pl
from jax.experimental.pallas import tpu as pltpu

D_MODEL = 1024
BATCH = 8
SEQ = 2048
EPS = 1e-6
CHUNK = 128
E_A = 2 * D_MODEL
G_A = 8
DG_A = E_A // G_A
E_B = 3 * D_MODEL // 2
H_B = 16
BD_B = E_B // H_B
CONV_W = 4
LRU_C = 8.0
H_C = 16
DH_C = D_MODEL // H_C
E_C = H_C * DH_C

LANES = 128
VMEM_LIMIT = 56 * 1024 * 1024

TM_A = 512
T_B = 64
HG_B = 4
GW_B = HG_B * BD_B
TM_C = 512
TQ_C = 256
TK_C = 256
NH_C = 4
DEN_ROWS = 16
NEG_BIG = -1e30
LOG2E = 1.4426950408889634

F32 = jnp.float32
BF16 = jnp.bfloat16


def _dot(a, b):
    return jnp.dot(a, b, preferred_element_type=F32)


def _rms(x, g):
    ms = jnp.mean(x * x, axis=-1, keepdims=True)
    return x * lax.rsqrt(ms + EPS) * g


def _gelu(x):
    return x * (0.5 * (1.0 + jnp.tanh(0.7978845608028654 * (x + 0.044715 * (x * x * x)))))


def _sigmoid(x):
    return 0.5 * (1.0 + jnp.tanh(0.5 * x))


def _silu(x):
    return x * _sigmoid(x)


def _softplus(x):
    return jnp.maximum(x, 0.0) + jnp.log1p(jnp.exp(-jnp.abs(x)))


def _const_spec(shape):
    n = len(shape)
    return pl.BlockSpec(shape, lambda *_: (0,) * n, pipeline_mode=pl.Buffered(1))


def _gmlp_kernel(x_ref, nrm_ref, win_ref, vnrm_ref, ws_ref, bst_ref, wo_ref, o_ref, vn_scr, y_scr):
    x = x_ref[...]
    h = _rms(x, nrm_ref[...]).astype(BF16)
    v = _gelu(_dot(h, win_ref[:, E_A:2 * E_A]))
    vn_scr[...] = _rms(v, vnrm_ref[...]).astype(BF16)
    row = lax.broadcasted_iota(jnp.int32, (CHUNK, CHUNK), 0)
    col = lax.broadcasted_iota(jnp.int32, (CHUNK, CHUNK), 1)
    causal = col <= row
    for g in range(G_A):
        c0 = g * DG_A
        u = _gelu(_dot(h, win_ref[:, c0:c0 + DG_A]))
        gate = _silu(_dot(h, win_ref[:, 2 * E_A + c0:2 * E_A + c0 + DG_A]))
        w = jnp.where(causal, ws_ref[g], 0.0).astype(BF16)
        bias = bst_ref[:, g:g + 1]
        for c in range(TM_A // CHUNK):
            r0 = c * CHUNK
            mixed = _dot(w, vn_scr[r0:r0 + CHUNK, c0:c0 + DG_A]) + bias
            y = u[r0:r0 + CHUNK] * mixed * gate[r0:r0 + CHUNK]
            y_scr[r0:r0 + CHUNK, c0:c0 + DG_A] = y.astype(BF16)
    o_ref[...] = x + _dot(y_scr[...], wo_ref[...])


def _gmlp_layer(x2, norm, w_in, v_norm, w_s, b_s, w_out):
    n = x2.shape[0]
    return pl.pallas_call(
        _gmlp_kernel,
        grid=(n // TM_A,),
        in_specs=[
            pl.BlockSpec((TM_A, D_MODEL), lambda i: (i, 0)),
            _const_spec((1, D_MODEL)),
            _const_spec((D_MODEL, 3 * E_A)),
            _const_spec((1, E_A)),
            _const_spec((G_A, CHUNK, CHUNK)),
            _const_spec((CHUNK, G_A)),
            _const_spec((E_A, D_MODEL)),
        ],
        out_specs=pl.BlockSpec((TM_A, D_MODEL), lambda i: (i, 0)),
        out_shape=jax.ShapeDtypeStruct((n, D_MODEL), F32),
        scratch_shapes=[pltpu.VMEM((TM_A, E_A), BF16), pltpu.VMEM((TM_A, E_A), BF16)],
        compiler_params=pltpu.CompilerParams(dimension_semantics=("arbitrary",), vmem_limit_bytes=VMEM_LIMIT),
        name="gmlp_layer",
    )(x2, norm.reshape(1, D_MODEL), w_in.astype(BF16), v_norm.reshape(1, E_A), w_s, b_s.T, w_out.astype(BF16))


ROWS_B = T_B * BATCH
TAIL_B = (CONV_W - 1) * BATCH


def _rglru_kernel(x_ref, nrm_ref, win_ref, cw_ref, cb_ref, wbd_ref, ba_ref, bx_ref, lam_ref, wo_ref, o_ref,
                  hn_scr, xb_scr, a_scr, hs_scr, state_scr, out_scr):
    @pl.when(pl.program_id(0) == 0)
    def _():
        xb_scr[0:TAIL_B, :] = jnp.zeros((TAIL_B, E_B), F32)
        state_scr[...] = jnp.zeros((BATCH, E_B), F32)

    nrm = nrm_ref[...]
    for b in range(BATCH):
        hb = _rms(x_ref[b], nrm)
        for c in range(D_MODEL // LANES):
            hn_scr[c, pl.ds(b, T_B, stride=BATCH), :] = hb[:, c * LANES:(c + 1) * LANES]
    h = jnp.concatenate([hn_scr[c] for c in range(D_MODEL // LANES)], axis=1).astype(BF16)
    xb_scr[TAIL_B:TAIL_B + ROWS_B, :] = _dot(h, win_ref[:, 0:E_B])
    gate = _silu(_dot(h, win_ref[:, E_B:2 * E_B]))

    xc = cb_ref[...] + cw_ref[0:1, :] * xb_scr[0:ROWS_B, :]
    for k in range(1, CONV_W):
        xc = xc + cw_ref[k:k + 1, :] * xb_scr[k * BATCH:k * BATCH + ROWS_B, :]
    tail = xb_scr[ROWS_B:ROWS_B + TAIL_B, :]
    xb_scr[0:TAIL_B, :] = tail
    xcb = xc.astype(BF16)

    neg_c_sp = -LRU_C * _softplus(-lam_ref[...])
    for j in range(E_B // GW_B):
        c0 = j * GW_B
        pre = _dot(xcb[:, c0:c0 + GW_B], wbd_ref[j])
        r = _sigmoid(pre[:, 0:GW_B] + ba_ref[:, c0:c0 + GW_B])
        i = _sigmoid(pre[:, GW_B:2 * GW_B] + bx_ref[:, c0:c0 + GW_B])
        log_a = neg_c_sp[:, c0:c0 + GW_B] * r
        t = jnp.tanh(0.5 * log_a)
        d = 1.0 / (1.0 - t)
        a_scr[:, c0:c0 + GW_B] = (1.0 + t) * d
        hs_scr[:, c0:c0 + GW_B] = (2.0 * d) * jnp.sqrt(-t) * (i * xc[:, c0:c0 + GW_B])

    def step(t, hprev):
        r0 = pl.multiple_of(t * BATCH, BATCH)
        hnew = a_scr[pl.ds(r0, BATCH), :] * hprev + hs_scr[pl.ds(r0, BATCH), :]
        hs_scr[pl.ds(r0, BATCH), :] = hnew
        return hnew

    state_scr[...] = lax.fori_loop(0, T_B, step, state_scr[...], unroll=8)

    y = (hs_scr[...] * gate).astype(BF16)
    out = _dot(y, wo_ref[...])
    for c in range(D_MODEL // LANES):
        out_scr[c] = out[:, c * LANES:(c + 1) * LANES]
    for b in range(BATCH):
        ob = jnp.concatenate([out_scr[c, pl.ds(b, T_B, stride=BATCH), :] for c in range(D_MODEL // LANES)], axis=1)
        o_ref[b] = x_ref[b] + ob


def _block_diag_gates(w_a, w_x):
    def bd(w):
        wg = w.reshape(H_B // HG_B, HG_B, BD_B, BD_B)
        eye = jnp.eye(HG_B, dtype=w.dtype)
        return jnp.einsum('ghij,hk->ghikj', wg, eye).reshape(H_B // HG_B, GW_B, GW_B)
    return jnp.concatenate([bd(w_a), bd(w_x)], axis=-1)


def _rglru_layer(x3, norm, w_in, conv_w, conv_b, w_a, b_a, w_x, b_x, lam, w_out):
    ng = H_B // HG_B
    return pl.pallas_call(
        _rglru_kernel,
        grid=(SEQ // T_B,),
        in_specs=[
            pl.BlockSpec((BATCH, T_B, D_MODEL), lambda i: (0, i, 0)),
            _const_spec((1, D_MODEL)),
            _const_spec((D_MODEL, 2 * E_B)),
            _const_spec((CONV_W, E_B)),
            _const_spec((1, E_B)),
            _const_spec((ng, GW_B, 2 * GW_B)),
            _const_spec((1, E_B)),
            _const_spec((1, E_B)),
            _const_spec((1, E_B)),
            _const_spec((E_B, D_MODEL)),
        ],
        out_specs=pl.BlockSpec((BATCH, T_B, D_MODEL), lambda i: (0, i, 0)),
        out_shape=jax.ShapeDtypeStruct((BATCH, SEQ, D_MODEL), F32),
        scratch_shapes=[
            pltpu.VMEM((D_MODEL // LANES, ROWS_B, LANES), F32),
            pltpu.VMEM((TAIL_B + ROWS_B, E_B), F32),
            pltpu.VMEM((ROWS_B, E_B), F32),
            pltpu.VMEM((ROWS_B, E_B), F32),
            pltpu.VMEM((BATCH, E_B), F32),
            pltpu.VMEM((D_MODEL // LANES, ROWS_B, LANES), F32),
        ],
        compiler_params=pltpu.CompilerParams(dimension_semantics=("arbitrary",), vmem_limit_bytes=VMEM_LIMIT),
        name="rglru_layer",
    )(x3, norm.reshape(1, D_MODEL), w_in.astype(BF16), conv_w, conv_b.reshape(1, E_B),
      _block_diag_gates(w_a, w_x).astype(BF16), b_a.reshape(1, E_B), b_x.reshape(1, E_B), lam.reshape(1, E_B),
      w_out.astype(BF16))


N_SPLIT = 3


def _fox_proj_kernel(x_ref, nrm_ref, wqkg_ref, wvt_ref, wf_ref, bf_ref, qg_ref, kg_ref, seg_ref, segt_ref, place_ref,
                     q_ref, k_ref, ek_ref, vt_ref, sg_ref, cum_ref, carry_scr):
    @pl.when(pl.program_id(1) == 0)
    def _():
        carry_scr[...] = jnp.zeros((1, LANES), F32)

    h = _rms(x_ref[0], nrm_ref[...]).astype(BF16)

    def head_norm(z, gain):
        ss = _dot((z * z).astype(BF16), seg_ref[...])
        r = lax.rsqrt(ss * (1.0 / DH_C) + EPS)
        r_hi = r.astype(BF16)
        r_lo = (r - r_hi.astype(F32)).astype(BF16)
        rexp = _dot(r_hi, segt_ref[...]) + _dot(r_lo, segt_ref[...])
        return z * rexp * gain

    q = head_norm(_dot(h, wqkg_ref[:, 0:E_C]), qg_ref[...])
    q_ref[0] = q.astype(BF16)
    k = head_norm(_dot(h, wqkg_ref[:, E_C:2 * E_C]), kg_ref[...])
    k_ref[0] = k.astype(BF16)
    sg_ref[0] = _silu(_dot(h, wqkg_ref[:, 2 * E_C:3 * E_C])).astype(BF16)
    vt_ref[0] = lax.dot_general(wvt_ref[...], h, (((1,), (1,)), ((), ())), preferred_element_type=F32).astype(BF16)

    z = _dot(h, wf_ref[...]) + bf_ref[...]
    log_f = jnp.minimum(z, 0.0) - jnp.log1p(jnp.exp(-jnp.abs(z)))
    row = lax.broadcasted_iota(jnp.int32, (CHUNK, CHUNK), 0)
    col = lax.broadcasted_iota(jnp.int32, (CHUNK, CHUNK), 1)
    tri = jnp.where(col <= row, 1.0, 0.0).astype(F32)
    group = lax.broadcasted_iota(jnp.int32, (CHUNK, LANES), 1) // H_C
    carry = carry_scr[...]
    for c in range(TM_C // CHUNK):
        r0 = c * CHUNK
        cum = jnp.dot(tri, log_f[r0:r0 + CHUNK], preferred_element_type=F32,
                      precision=lax.Precision.HIGHEST) + carry
        carry = cum[CHUNK - 1:CHUNK, :]
        c2 = cum * LOG2E
        cum_ref[0, r0:r0 + CHUNK, :] = c2[:, 0:H_C]
        hi = c2.astype(BF16)
        r1 = c2 - hi.astype(F32)
        mid = r1.astype(BF16)
        lo = (r1 - mid.astype(F32)).astype(BF16)
        parts = jnp.where(group == 0, hi, jnp.where(group == 1, mid, lo))
        ek_ref[0, r0:r0 + CHUNK, :] = _dot(parts, place_ref[...]).astype(BF16)
    carry_scr[...] = carry


def _fox_proj(x3, norm, w_in, b_f, q_norm, k_norm):
    w_qkg = jnp.concatenate([w_in[:, 0:2 * E_C], w_in[:, 3 * E_C:4 * E_C]], axis=1).astype(BF16)
    w_vt = w_in[:, 2 * E_C:3 * E_C].T.astype(BF16)
    pad = LANES - N_SPLIT * H_C
    w_f = jnp.pad(jnp.tile(w_in[:, 4 * E_C:], (1, N_SPLIT)), ((0, 0), (0, pad))).astype(BF16)
    b_fp = jnp.pad(jnp.tile(b_f, N_SPLIT), (0, pad)).reshape(1, LANES)
    qg = jnp.tile(q_norm, H_C).reshape(1, E_C) * (DH_C ** -0.5 * LOG2E)
    kg = jnp.tile(k_norm, H_C).reshape(1, E_C)
    head_of = jnp.arange(E_C) // DH_C
    seg = (head_of[:, None] == jnp.arange(LANES)[None, :]).astype(BF16)
    src = jnp.arange(LANES)
    dst = (src % H_C) * DH_C + src // H_C
    place = jnp.where((src[:, None] < N_SPLIT * H_C) & (dst[:, None] == jnp.arange(E_C)[None, :]), -1.0, 0.0)
    tok = jax.ShapeDtypeStruct((BATCH, SEQ, E_C), BF16)
    blk = pl.BlockSpec((1, TM_C, E_C), lambda b, i: (b, i, 0))
    return pl.pallas_call(
        _fox_proj_kernel,
        grid=(BATCH, SEQ // TM_C),
        in_specs=[
            pl.BlockSpec((1, TM_C, D_MODEL), lambda b, i: (b, i, 0)),
            _const_spec((1, D_MODEL)),
            _const_spec((D_MODEL, 3 * E_C)),
            _const_spec((E_C, D_MODEL)),
            _const_spec((D_MODEL, LANES)),
            _const_spec((1, LANES)),
            _const_spec((1, E_C)),
            _const_spec((1, E_C)),
            _const_spec((E_C, LANES)),
            _const_spec((LANES, E_C)),
            _const_spec((LANES, E_C)),
        ],
        out_specs=[blk, blk, blk, pl.BlockSpec((1, E_C, TM_C), lambda b, i: (b, 0, i)), blk,
                   pl.BlockSpec((1, TM_C, H_C), lambda b, i: (b, i, 0))],
        out_shape=[tok, tok, tok, jax.ShapeDtypeStruct((BATCH, E_C, SEQ), BF16), tok,
                   jax.ShapeDtypeStruct((BATCH, SEQ, H_C), F32)],
        scratch_shapes=[pltpu.VMEM((1, LANES), F32)],
        compiler_params=pltpu.CompilerParams(dimension_semantics=("arbitrary", "arbitrary"),
                                             vmem_limit_bytes=VMEM_LIMIT),
        name="fox_proj",
    )(x3, norm.reshape(1, D_MODEL), w_qkg, w_vt, w_f, b_fp, qg, kg, seg, seg.T, place.astype(BF16))


def _attn_kernel(q_ref, k_ref, ek_ref, vt_ref, sg_ref, cq_ref, o_ref,
                 qa_scr, acc_scr, m_scr, sta_scr, stb_scr, mxa_scr, mxb_scr):
    lane_q = lax.broadcasted_iota(jnp.int32, (TQ_C, LANES), 1)
    first_q = lane_q < DH_C
    extra_q = lane_q % DH_C < N_SPLIT
    causal = (lax.broadcasted_iota(jnp.int32, (TK_C, TQ_C), 0) <= lax.broadcasted_iota(jnp.int32, (TK_C, TQ_C), 1))
    ones_v = jnp.ones((DEN_ROWS, TK_C), BF16)
    heads = range(NH_C)

    def load_queries(qi):
        rows = slice(qi * TQ_C, (qi + 1) * TQ_C)
        for hh in heads:
            pair, odd = divmod(hh, 2)
            qf = q_ref[0, rows, pair * LANES:(pair + 1) * LANES].astype(F32)
            mine = jnp.logical_not(first_q) if odd else first_q
            qa_scr[hh, :, 0:LANES] = jnp.where(mine, qf, 0.0).astype(BF16)
            qa_scr[hh, :, LANES:2 * LANES] = jnp.where(jnp.logical_and(mine, extra_q), 1.0, 0.0).astype(BF16)

    def scores(qi, j, st_ref, mx_ref):
        keys = slice(j * TK_C, (j + 1) * TK_C)
        for hh in heads:
            pair = hh // 2
            ka = jnp.concatenate([k_ref[0, keys, pair * LANES:(pair + 1) * LANES],
                                  ek_ref[0, keys, pair * LANES:(pair + 1) * LANES]], axis=1)
            st = lax.dot_general(ka, qa_scr[hh], (((1,), (1,)), ((), ())), preferred_element_type=F32)
            if j == qi:
                st = jnp.where(causal, st, NEG_BIG)
            st_ref[hh] = st
            mx_ref[hh] = jnp.max(st, axis=0, keepdims=True)

    def consume(qi, j, st_ref, mx_ref):
        alpha, pt = [], []
        for hh in heads:
            cq = cq_ref[0, 0, hh:hh + 1, qi * TQ_C:(qi + 1) * TQ_C]
            m_old = m_scr[hh]
            m = jnp.maximum(m_old, mx_ref[hh] + cq)
            pt.append(jnp.exp2(st_ref[hh] - (m - cq)).astype(BF16))
            alpha.append(jnp.exp2(m_old - m))
            m_scr[hh] = m
        pv = []
        for hh in heads:
            vt = vt_ref[0, hh * DH_C:(hh + 1) * DH_C, j * TK_C:(j + 1) * TK_C]
            pv.append(_dot(jnp.concatenate([vt, ones_v], axis=0), pt[hh]))
        for hh in heads:
            acc_scr[hh] = alpha[hh] * acc_scr[hh] + pv[hh]

    def finish(qi):
        rows = slice(qi * TQ_C, (qi + 1) * TQ_C)
        for pair in range(NH_C // 2):
            a0 = acc_scr[2 * pair]
            a1 = acc_scr[2 * pair + 1]
            ot = jnp.concatenate([a0[0:DH_C] / a0[DH_C:DH_C + 1], a1[0:DH_C] / a1[DH_C:DH_C + 1]], axis=0)
            cols = slice(pair * LANES, (pair + 1) * LANES)
            o_ref[0, rows, cols] = (ot.T * sg_ref[0, rows, cols].astype(F32)).astype(BF16)

    tiles = [(qi, j) for qi in range(SEQ // TQ_C) for j in range(qi + 1)]
    bufs = ((sta_scr, mxa_scr), (stb_scr, mxb_scr))
    def issue(n):
        qi, j = tiles[n]
        if j == 0:
            load_queries(qi)
        scores(qi, j, *bufs[n % 2])

    issue(0)
    issue(1)
    for n, (qi, j) in enumerate(tiles):
        if j == 0:
            acc_scr[...] = jnp.zeros(acc_scr.shape, F32)
            m_scr[...] = jnp.full(m_scr.shape, NEG_BIG, F32)
        consume(qi, j, *bufs[n % 2])
        if j == qi:
            finish(qi)
        if n + 2 < len(tiles):
            issue(n + 2)


def _fox_attention(q, k, ek, vt, sg, cum):
    assert TQ_C == TK_C
    ng = H_C // NH_C
    w = NH_C * DH_C
    cq = cum.transpose(0, 2, 1).reshape(BATCH, ng, NH_C, SEQ)
    blk = pl.BlockSpec((1, SEQ, w), lambda b, g: (b, 0, g))
    return pl.pallas_call(
        _attn_kernel,
        grid=(BATCH, ng),
        in_specs=[
            blk, blk, blk,
            pl.BlockSpec((1, w, SEQ), lambda b, g: (b, g, 0)),
            blk,
            pl.BlockSpec((1, 1, NH_C, SEQ), lambda b, g: (b, g, 0, 0)),
        ],
        out_specs=blk,
        out_shape=jax.ShapeDtypeStruct((BATCH, SEQ, E_C), BF16),
        scratch_shapes=[pltpu.VMEM((NH_C, TQ_C, 2 * LANES), BF16), pltpu.VMEM((NH_C, DH_C + DEN_ROWS, TQ_C), F32),
                        pltpu.VMEM((NH_C, 1, TQ_C), F32),
                        pltpu.VMEM((NH_C, TK_C, TQ_C), F32), pltpu.VMEM((NH_C, TK_C, TQ_C), F32),
                        pltpu.VMEM((NH_C, 1, TQ_C), F32), pltpu.VMEM((NH_C, 1, TQ_C), F32)],
        compiler_params=pltpu.CompilerParams(dimension_semantics=("arbitrary", "arbitrary"),
                                             vmem_limit_bytes=VMEM_LIMIT),
        name="fox_attention",
    )(q, k, ek, vt, sg, cq)


def _out_proj_kernel(y_ref, x_ref, wo_ref, o_ref):
    o_ref[...] = x_ref[...] + _dot(y_ref[...], wo_ref[...])


def _fox_out_proj(y2, x2, w_out):
    n = x2.shape[0]
    return pl.pallas_call(
        _out_proj_kernel,
        grid=(n // TM_C,),
        in_specs=[
            pl.BlockSpec((TM_C, E_C), lambda i: (i, 0)),
            pl.BlockSpec((TM_C, D_MODEL), lambda i: (i, 0)),
            _const_spec((E_C, D_MODEL)),
        ],
        out_specs=pl.BlockSpec((TM_C, D_MODEL), lambda i: (i, 0)),
        out_shape=jax.ShapeDtypeStruct((n, D_MODEL), F32),
        compiler_params=pltpu.CompilerParams(dimension_semantics=("arbitrary",), vmem_limit_bytes=VMEM_LIMIT),
        name="fox_out_proj",
    )(y2, x2, w_out.astype(BF16))


def _fox_layer(x3, norm, w_in, b_f, q_norm, k_norm, w_out):
    q, k, ek, vt, sg, cum = _fox_proj(x3, norm, w_in, b_f, q_norm, k_norm)
    y = _fox_attention(q, k, ek, vt, sg, cum)
    n = BATCH * SEQ
    return _fox_out_proj(y.reshape(n, E_C), x3.reshape(n, D_MODEL), w_out).reshape(BATCH, SEQ, D_MODEL)


def kernel(x, l0_norm, l0_w_in, l0_v_norm, l0_w_s, l0_b_s, l0_w_out, l1_norm, l1_w_in, l1_conv_w, l1_conv_b, l1_w_a, l1_b_a, l1_w_x, l1_b_x, l1_lam, l1_w_out, l2_norm, l2_w_in, l2_b_f, l2_q_norm, l2_k_norm, l2_w_out, l3_norm, l3_w_in, l3_v_norm, l3_w_s, l3_b_s, l3_w_out):
    n = BATCH * SEQ
    x = _gmlp_layer(x.reshape(n, D_MODEL), l0_norm, l0_w_in, l0_v_norm, l0_w_s, l0_b_s, l0_w_out)
    x = _rglru_layer(x.reshape(BATCH, SEQ, D_MODEL), l1_norm, l1_w_in, l1_conv_w, l1_conv_b, l1_w_a, l1_b_a,
                     l1_w_x, l1_b_x, l1_lam, l1_w_out)
    x = _fox_layer(x, l2_norm, l2_w_in, l2_b_f, l2_q_norm, l2_k_norm, l2_w_out)
    x = _gmlp_layer(x.reshape(n, D_MODEL), l3_norm, l3_w_in, l3_v_norm, l3_w_s, l3_b_s, l3_w_out)
    return x.reshape(BATCH, SEQ, D_MODEL)
```

```python
import jax
import jax.numpy as jnp
from jax import lax
from jax.experimental import pallas as pl
from jax.experimental.pallas import tpu as pltpu

D_MODEL = 1024
BATCH = 8
SEQ = 2048
EPS = 1e-6
CHUNK = 128
E_A = 2 * D_MODEL
G_A = 8
DG_A = E_A // G_A
E_B = 3 * D_MODEL // 2
H_B = 16
BD_B = E_B // H_B
CONV_W = 4
LRU_C = 8.0
H_C = 16
DH_C = D_MODEL // H_C
E_C = H_C * DH_C

LANES = 128
VMEM_LIMIT = 56 * 1024 * 1024

TM_A = 512
T_B = 64
HG_B = 4
GW_B = HG_B * BD_B
TM_C = 512
TQ_C = 256
TK_C = 256
NH_C = 4
DEN_ROWS = 16
NEG_BIG = -1e30
LOG2E = 1.4426950408889634

F32 = jnp.float32
BF16 = jnp.bfloat16


def _dot(a, b):
    return jnp.dot(a, b, preferred_element_type=F32)


def _rms(x, g):
    ms = jnp.mean(x * x, axis=-1, keepdims=True)
    return x * lax.rsqrt(ms + EPS) * g


def _gelu(x):
    return x * (0.5 * (1.0 + jnp.tanh(0.7978845608028654 * (x + 0.044715 * (x * x * x)))))


def _sigmoid(x):
    return 0.5 * (1.0 + jnp.tanh(0.5 * x))


def _silu(x):
    return x * _sigmoid(x)


def _softplus(x):
    return jnp.maximum(x, 0.0) + jnp.log1p(jnp.exp(-jnp.abs(x)))


def _const_spec(shape):
    n = len(shape)
    return pl.BlockSpec(shape, lambda *_: (0,) * n, pipeline_mode=pl.Buffered(1))


def _gmlp_kernel(x_ref, nrm_ref, win_ref, vnrm_ref, ws_ref, bst_ref, wo_ref, o_ref, vn_scr, y_scr):
    x = x_ref[...]
    h = _rms(x, nrm_ref[...]).astype(BF16)
    v = _gelu(_dot(h, win_ref[:, E_A:2 * E_A]))
    vn_scr[...] = _rms(v, vnrm_ref[...]).astype(BF16)
    row = lax.broadcasted_iota(jnp.int32, (CHUNK, CHUNK), 0)
    col = lax.broadcasted_iota(jnp.int32, (CHUNK, CHUNK), 1)
    causal = col <= row
    for g in range(G_A):
        c0 = g * DG_A
        u = _gelu(_dot(h, win_ref[:, c0:c0 + DG_A]))
        gate = _silu(_dot(h, win_ref[:, 2 * E_A + c0:2 * E_A + c0 + DG_A]))
        w = jnp.where(causal, ws_ref[g], 0.0).astype(BF16)
        bias = bst_ref[:, g:g + 1]
        for c in range(TM_A // CHUNK):
            r0 = c * CHUNK
            mixed = _dot(w, vn_scr[r0:r0 + CHUNK, c0:c0 + DG_A]) + bias
            y = u[r0:r0 + CHUNK] * mixed * gate[r0:r0 + CHUNK]
            y_scr[r0:r0 + CHUNK, c0:c0 + DG_A] = y.astype(BF16)
    o_ref[...] = x + _dot(y_scr[...], wo_ref[...])


def _gmlp_layer(x2, norm, w_in, v_norm, w_s, b_s, w_out):
    n = x2.shape[0]
    return pl.pallas_call(
        _gmlp_kernel,
        grid=(n // TM_A,),
        in_specs=[
            pl.BlockSpec((TM_A, D_MODEL), lambda i: (i, 0)),
            _const_spec((1, D_MODEL)),
            _const_spec((D_MODEL, 3 * E_A)),
            _const_spec((1, E_A)),
            _const_spec((G_A, CHUNK, CHUNK)),
            _const_spec((CHUNK, G_A)),
            _const_spec((E_A, D_MODEL)),
        ],
        out_specs=pl.BlockSpec((TM_A, D_MODEL), lambda i: (i, 0)),
        out_shape=jax.ShapeDtypeStruct((n, D_MODEL), F32),
        scratch_shapes=[pltpu.VMEM((TM_A, E_A), BF16), pltpu.VMEM((TM_A, E_A), BF16)],
        compiler_params=pltpu.CompilerParams(dimension_semantics=("arbitrary",), vmem_limit_bytes=VMEM_LIMIT),
        name="gmlp_layer",
    )(x2, norm.reshape(1, D_MODEL), w_in.astype(BF16), v_norm.reshape(1, E_A), w_s, b_s.T, w_out.astype(BF16))


ROWS_B = T_B * BATCH
TAIL_B = (CONV_W - 1) * BATCH


def _rglru_kernel(x_ref, nrm_ref, win_ref, cw_ref, cb_ref, wbd_ref, ba_ref, bx_ref, lam_ref, wo_ref, o_ref,
                  hn_scr, xb_scr, a_scr, hs_scr, state_scr, out_scr):
    @pl.when(pl.program_id(0) == 0)
    def _():
        xb_scr[0:TAIL_B, :] = jnp.zeros((TAIL_B, E_B), F32)
        state_scr[...] = jnp.zeros((BATCH, E_B), F32)

    nrm = nrm_ref[...]
    for b in range(BATCH):
        hb = _rms(x_ref[b], nrm)
        for c in range(D_MODEL // LANES):
            hn_scr[c, pl.ds(b, T_B, stride=BATCH), :] = hb[:, c * LANES:(c + 1) * LANES]
    h = jnp.concatenate([hn_scr[c] for c in range(D_MODEL // LANES)], axis=1).astype(BF16)
    xb_scr[TAIL_B:TAIL_B + ROWS_B, :] = _dot(h, win_ref[:, 0:E_B])
    gate = _silu(_dot(h, win_ref[:, E_B:2 * E_B]))

    xc = cb_ref[...] + cw_ref[0:1, :] * xb_scr[0:ROWS_B, :]
    for k in range(1, CONV_W):
        xc = xc + cw_ref[k:k + 1, :] * xb_scr[k * BATCH:k * BATCH + ROWS_B, :]
    tail = xb_scr[ROWS_B:ROWS_B + TAIL_B, :]
    xb_scr[0:TAIL_B, :] = tail
    xcb = xc.astype(BF16)

    neg_c_sp = -LRU_C * _softplus(-lam_ref[...])
    for j in range(E_B // GW_B):
        c0 = j * GW_B
        pre = _dot(xcb[:, c0:c0 + GW_B], wbd_ref[j])
        r = _sigmoid(pre[:, 0:GW_B] + ba_ref[:, c0:c0 + GW_B])
        i = _sigmoid(pre[:, GW_B:2 * GW_B] + bx_ref[:, c0:c0 + GW_B])
        log_a = neg_c_sp[:, c0:c0 + GW_B] * r
        t = jnp.tanh(0.5 * log_a)
        d = 1.0 / (1.0 - t)
        a_scr[:, c0:c0 + GW_B] = (1.0 + t) * d
        hs_scr[:, c0:c0 + GW_B] = (2.0 * d) * jnp.sqrt(-t) * (i * xc[:, c0:c0 + GW_B])

    hcur = state_scr[...]
    for t in range(T_B):
        rows = slice(t * BATCH, (t + 1) * BATCH)
        hcur = a_scr[rows, :] * hcur + hs_scr[rows, :]
        hs_scr[rows, :] = hcur
    state_scr[...] = hcur

    y = (hs_scr[...] * gate).astype(BF16)
    out = _dot(y, wo_ref[...])
    for c in range(D_MODEL // LANES):
        out_scr[c] = out[:, c * LANES:(c + 1) * LANES]
    for b in range(BATCH):
        ob = jnp.concatenate([out_scr[c, pl.ds(b, T_B, stride=BATCH), :] for c in range(D_MODEL // LANES)], axis=1)
        o_ref[b] = x_ref[b] + ob


def _block_diag_gates(w_a, w_x):
    def bd(w):
        wg = w.reshape(H_B // HG_B, HG_B, BD_B, BD_B)
        eye = jnp.eye(HG_B, dtype=w.dtype)
        return jnp.einsum('ghij,hk->ghikj', wg, eye).reshape(H_B // HG_B, GW_B, GW_B)
    return jnp.concatenate([bd(w_a), bd(w_x)], axis=-1)


def _rglru_layer(x3, norm, w_in, conv_w, conv_b, w_a, b_a, w_x, b_x, lam, w_out):
    ng = H_B // HG_B
    return pl.pallas_call(
        _rglru_kernel,
        grid=(SEQ // T_B,),
        in_specs=[
            pl.BlockSpec((BATCH, T_B, D_MODEL), lambda i: (0, i, 0)),
            _const_spec((1, D_MODEL)),
            _const_spec((D_MODEL, 2 * E_B)),
            _const_spec((CONV_W, E_B)),
            _const_spec((1, E_B)),
            _const_spec((ng, GW_B, 2 * GW_B)),
            _const_spec((1, E_B)),
            _const_spec((1, E_B)),
            _const_spec((1, E_B)),
            _const_spec((E_B, D_MODEL)),
        ],
        out_specs=pl.BlockSpec((BATCH, T_B, D_MODEL), lambda i: (0, i, 0)),
        out_shape=jax.ShapeDtypeStruct((BATCH, SEQ, D_MODEL), F32),
        scratch_shapes=[
            pltpu.VMEM((D_MODEL // LANES, ROWS_B, LANES), F32),
            pltpu.VMEM((TAIL_B + ROWS_B, E_B), F32),
            pltpu.VMEM((ROWS_B, E_B), F32),
            pltpu.VMEM((ROWS_B, E_B), F32),
            pltpu.VMEM((BATCH, E_B), F32),
            pltpu.VMEM((D_MODEL // LANES, ROWS_B, LANES), F32),
        ],
        compiler_params=pltpu.CompilerParams(dimension_semantics=("arbitrary",), vmem_limit_bytes=VMEM_LIMIT),
        name="rglru_layer",
    )(x3, norm.reshape(1, D_MODEL), w_in.astype(BF16), conv_w, conv_b.reshape(1, E_B),
      _block_diag_gates(w_a, w_x).astype(BF16), b_a.reshape(1, E_B), b_x.reshape(1, E_B), lam.reshape(1, E_B),
      w_out.astype(BF16))


N_SPLIT = 3


def _fox_proj_kernel(x_ref, nrm_ref, wkg_ref, wqvt_ref, wf_ref, bf_ref, kg_ref, seg_ref, segt_ref, place_ref,
                     qt_ref, k_ref, ek_ref, vt_ref, sg_ref, cum_ref, carry_scr):
    @pl.when(pl.program_id(1) == 0)
    def _():
        carry_scr[...] = jnp.zeros((1, LANES), F32)

    h = _rms(x_ref[0], nrm_ref[...]).astype(BF16)

    zk = _dot(h, wkg_ref[:, 0:E_C])
    ss = _dot((zk * zk).astype(BF16), seg_ref[...])
    r = lax.rsqrt(ss * (1.0 / DH_C) + EPS)
    r_hi = r.astype(BF16)
    r_lo = (r - r_hi.astype(F32)).astype(BF16)
    rexp = _dot(jnp.concatenate([r_hi, r_lo], axis=1), segt_ref[...])
    k_ref[0] = (zk * rexp * kg_ref[...]).astype(BF16)
    sg_ref[0] = _silu(_dot(h, wkg_ref[:, E_C:2 * E_C])).astype(BF16)

    zqv = lax.dot_general(wqvt_ref[...], h, (((1,), (1,)), ((), ())), preferred_element_type=F32)
    vt_ref[0] = zqv[E_C:2 * E_C].astype(BF16)
    for hd in range(H_C):
        rows = slice(hd * DH_C, (hd + 1) * DH_C)
        zq = zqv[rows]
        rq = lax.rsqrt(jnp.mean(zq * zq, axis=0, keepdims=True) + EPS)
        qt_ref[0, rows, :] = (zq * rq).astype(BF16)

    z = _dot(h, wf_ref[...]) + bf_ref[...]
    log_f = jnp.minimum(z, 0.0) - jnp.log1p(jnp.exp(-jnp.abs(z)))
    row = lax.broadcasted_iota(jnp.int32, (CHUNK, CHUNK), 0)
    col = lax.broadcasted_iota(jnp.int32, (CHUNK, CHUNK), 1)
    tri = jnp.where(col <= row, 1.0, 0.0).astype(F32)
    group = lax.broadcasted_iota(jnp.int32, (CHUNK, LANES), 1) // H_C
    carry = carry_scr[...]
    for c in range(TM_C // CHUNK):
        r0 = c * CHUNK
        cum = jnp.dot(tri, log_f[r0:r0 + CHUNK], preferred_element_type=F32,
                      precision=lax.Precision.HIGHEST) + carry
        carry = cum[CHUNK - 1:CHUNK, :]
        c2 = cum * LOG2E
        cum_ref[0, r0:r0 + CHUNK, :] = c2[:, 0:H_C]
        hi = c2.astype(BF16)
        r1 = c2 - hi.astype(F32)
        mid = r1.astype(BF16)
        lo = (r1 - mid.astype(F32)).astype(BF16)
        parts = jnp.where(group == 0, hi, jnp.where(group == 1, mid, lo))
        ek_ref[0, r0:r0 + CHUNK, :] = _dot(parts, place_ref[...]).astype(BF16)
    carry_scr[...] = carry


def _fox_proj(x3, norm, w_in, b_f, q_norm, k_norm):
    w_kg = jnp.concatenate([w_in[:, E_C:2 * E_C], w_in[:, 3 * E_C:4 * E_C]], axis=1).astype(BF16)
    w_qvt = jnp.concatenate([w_in[:, 0:E_C], w_in[:, 2 * E_C:3 * E_C]], axis=1).T.astype(BF16)
    pad = LANES - N_SPLIT * H_C
    w_f = jnp.pad(jnp.tile(w_in[:, 4 * E_C:], (1, N_SPLIT)), ((0, 0), (0, pad))).astype(BF16)
    b_fp = jnp.pad(jnp.tile(b_f, N_SPLIT), (0, pad)).reshape(1, LANES)
    kg = jnp.tile(k_norm * q_norm, H_C).reshape(1, E_C) * (DH_C ** -0.5 * LOG2E)
    head_of = jnp.arange(E_C) // DH_C
    seg = (head_of[:, None] == jnp.arange(LANES)[None, :]).astype(BF16)
    segt2 = jnp.concatenate([seg.T, seg.T], axis=0)
    src = jnp.arange(LANES)
    dst = (src % H_C) * DH_C + src // H_C
    place = jnp.where((src[:, None] < N_SPLIT * H_C) & (dst[:, None] == jnp.arange(E_C)[None, :]), -1.0, 0.0)
    tok = jax.ShapeDtypeStruct((BATCH, SEQ, E_C), BF16)
    tok_t = jax.ShapeDtypeStruct((BATCH, E_C, SEQ), BF16)
    blk = pl.BlockSpec((1, TM_C, E_C), lambda b, i: (b, i, 0))
    blk_t = pl.BlockSpec((1, E_C, TM_C), lambda b, i: (b, 0, i))
    return pl.pallas_call(
        _fox_proj_kernel,
        grid=(BATCH, SEQ // TM_C),
        in_specs=[
            pl.BlockSpec((1, TM_C, D_MODEL), lambda b, i: (b, i, 0)),
            _const_spec((1, D_MODEL)),
            _const_spec((D_MODEL, 2 * E_C)),
            _const_spec((2 * E_C, D_MODEL)),
            _const_spec((D_MODEL, LANES)),
            _const_spec((1, LANES)),
            _const_spec((1, E_C)),
            _const_spec((E_C, LANES)),
            _const_spec((2 * LANES, E_C)),
            _const_spec((LANES, E_C)),
        ],
        out_specs=[blk_t, blk, blk, blk_t, blk, pl.BlockSpec((1, TM_C, H_C), lambda b, i: (b, i, 0))],
        out_shape=[tok_t, tok, tok, tok_t, tok, jax.ShapeDtypeStruct((BATCH, SEQ, H_C), F32)],
        scratch_shapes=[pltpu.VMEM((1, LANES), F32)],
        compiler_params=pltpu.CompilerParams(dimension_semantics=("arbitrary", "arbitrary"),
                                             vmem_limit_bytes=VMEM_LIMIT),
        name="fox_proj",
    )(x3, norm.reshape(1, D_MODEL), w_kg, w_qvt, w_f, b_fp, kg, seg, segt2, place.astype(BF16))


def _attn_kernel(qt_ref, k_ref, ek_ref, vt_ref, sg_ref, cq_ref, o_ref,
                 qa_scr, acc_scr, m_scr, sta_scr, stb_scr, mxa_scr, mxb_scr):
    causal = (lax.broadcasted_iota(jnp.int32, (TK_C, TQ_C), 0) <= lax.broadcasted_iota(jnp.int32, (TK_C, TQ_C), 1))
    ones_v = jnp.ones((DEN_ROWS, TK_C), BF16)
    zero_q = jnp.zeros((DH_C, TQ_C), BF16)
    row_q = lax.broadcasted_iota(jnp.int32, (LANES, TQ_C), 0)
    heads = range(NH_C)

    def load_queries(qi):
        cols = slice(qi * TQ_C, (qi + 1) * TQ_C)
        for hh in heads:
            odd = hh % 2
            qt = qt_ref[0, hh * DH_C:(hh + 1) * DH_C, cols]
            qa_scr[hh, 0:LANES, :] = jnp.concatenate([zero_q, qt] if odd else [qt, zero_q], axis=0)
            mine = jnp.logical_and(row_q >= odd * DH_C, row_q < odd * DH_C + N_SPLIT)
            qa_scr[hh, LANES:2 * LANES, :] = jnp.where(mine, 1.0, 0.0).astype(BF16)

    def scores(qi, j, st_ref, mx_ref):
        keys = slice(j * TK_C, (j + 1) * TK_C)
        for hh in heads:
            pair = hh // 2
            ka = jnp.concatenate([k_ref[0, keys, pair * LANES:(pair + 1) * LANES],
                                  ek_ref[0, keys, pair * LANES:(pair + 1) * LANES]], axis=1)
            st = _dot(ka, qa_scr[hh])
            if j == qi:
                st = jnp.where(causal, st, NEG_BIG)
            st_ref[hh] = st
            mx_ref[hh] = jnp.max(st, axis=0, keepdims=True)

    def consume(qi, j, st_ref, mx_ref):
        alpha, pt = [], []
        for hh in heads:
            cq = cq_ref[0, 0, hh:hh + 1, qi * TQ_C:(qi + 1) * TQ_C]
            m_old = m_scr[hh]
            m = jnp.maximum(m_old, mx_ref[hh] + cq)
            pt.append(jnp.exp2(st_ref[hh] - (m - cq)).astype(BF16))
            alpha.append(jnp.exp2(m_old - m))
            m_scr[hh] = m
        pv = []
        for hh in heads:
            vt = vt_ref[0, hh * DH_C:(hh + 1) * DH_C, j * TK_C:(j + 1) * TK_C]
            pv.append(_dot(jnp.concatenate([vt, ones_v], axis=0), pt[hh]))
        for hh in heads:
            acc_scr[hh] = alpha[hh] * acc_scr[hh] + pv[hh]

    def finish(qi):
        rows = slice(qi * TQ_C, (qi + 1) * TQ_C)
        for pair in range(NH_C // 2):
            a0 = acc_scr[2 * pair]
            a1 = acc_scr[2 * pair + 1]
            ot = jnp.concatenate([a0[0:DH_C] / a0[DH_C:DH_C + 1], a1[0:DH_C] / a1[DH_C:DH_C + 1]], axis=0)
            cols = slice(pair * LANES, (pair + 1) * LANES)
            o_ref[0, rows, cols] = (ot.T * sg_ref[0, rows, cols].astype(F32)).astype(BF16)

    tiles = [(qi, j) for qi in range(SEQ // TQ_C) for j in range(qi + 1)]
    bufs = ((sta_scr, mxa_scr), (stb_scr, mxb_scr))
    def issue(n):
        qi, j = tiles[n]
        if j == 0:
            load_queries(qi)
        scores(qi, j, *bufs[n % 2])

    issue(0)
    issue(1)
    for n, (qi, j) in enumerate(tiles):
        if j == 0:
            acc_scr[...] = jnp.zeros(acc_scr.shape, F32)
            m_scr[...] = jnp.full(m_scr.shape, NEG_BIG, F32)
        consume(qi, j, *bufs[n % 2])
        if j == qi:
            finish(qi)
        if n + 2 < len(tiles):
            issue(n + 2)


def _fox_attention(qt, k, ek, vt, sg, cum):
    assert TQ_C == TK_C
    ng = H_C // NH_C
    w = NH_C * DH_C
    cq = cum.transpose(0, 2, 1).reshape(BATCH, ng, NH_C, SEQ)
    blk = pl.BlockSpec((1, SEQ, w), lambda b, g: (b, 0, g))
    blk_t = pl.BlockSpec((1, w, SEQ), lambda b, g: (b, g, 0))
    return pl.pallas_call(
        _attn_kernel,
        grid=(BATCH, ng),
        in_specs=[
            blk_t, blk, blk, blk_t, blk,
            pl.BlockSpec((1, 1, NH_C, SEQ), lambda b, g: (b, g, 0, 0)),
        ],
        out_specs=blk,
        out_shape=jax.ShapeDtypeStruct((BATCH, SEQ, E_C), BF16),
        scratch_shapes=[pltpu.VMEM((NH_C, 2 * LANES, TQ_C), BF16), pltpu.VMEM((NH_C, DH_C + DEN_ROWS, TQ_C), F32),
                        pltpu.VMEM((NH_C, 1, TQ_C), F32),
                        pltpu.VMEM((NH_C, TK_C, TQ_C), F32), pltpu.VMEM((NH_C, TK_C, TQ_C), F32),
                        pltpu.VMEM((NH_C, 1, TQ_C), F32), pltpu.VMEM((NH_C, 1, TQ_C), F32)],
        compiler_params=pltpu.CompilerParams(dimension_semantics=("arbitrary", "arbitrary"),
                                             vmem_limit_bytes=VMEM_LIMIT),
        name="fox_attention",
    )(qt, k, ek, vt, sg, cq)


def _out_proj_kernel(y_ref, x_ref, wo_ref, o_ref):
    o_ref[...] = x_ref[...] + _dot(y_ref[...], wo_ref[...])


def _fox_out_proj(y2, x2, w_out):
    n = x2.shape[0]
    return pl.pallas_call(
        _out_proj_kernel,
        grid=(n // TM_C,),
        in_specs=[
            pl.BlockSpec((TM_C, E_C), lambda i: (i, 0)),
            pl.BlockSpec((TM_C, D_MODEL), lambda i: (i, 0)),
            _const_spec((E_C, D_MODEL)),
        ],
        out_specs=pl.BlockSpec((TM_C, D_MODEL), lambda i: (i, 0)),
        out_shape=jax.ShapeDtypeStruct((n, D_MODEL), F32),
        compiler_params=pltpu.CompilerParams(dimension_semantics=("arbitrary",), vmem_limit_bytes=VMEM_LIMIT),
        name="fox_out_proj",
    )(y2, x2, w_out.astype(BF16))


def _fox_layer(x3, norm, w_in, b_f, q_norm, k_norm, w_out):
    qt, k, ek, vt, sg, cum = _fox_proj(x3, norm, w_in, b_f, q_norm, k_norm)
    y = _fox_attention(qt, k, ek, vt, sg, cum)
    n = BATCH * SEQ
    return _fox_out_proj(y.reshape(n, E_C), x3.reshape(n, D_MODEL), w_out).reshape(BATCH, SEQ, D_MODEL)


def kernel(x, l0_norm, l0_w_in, l0_v_norm, l0_w_s, l0_b_s, l0_w_out, l1_norm, l1_w_in, l1_conv_w, l1_conv_b, l1_w_a, l1_b_a, l1_w_x, l1_b_x, l1_lam, l1_w_out, l2_norm, l2_w_in, l2_b_f, l2_q_norm, l2_k_norm, l2_w_out, l3_norm, l3_w_in, l3_v_norm, l3_w_s, l3_b_s, l3_w_out):
    n = BATCH * SEQ
    x = _gmlp_layer(x.reshape(n, D_MODEL), l0_norm, l0_w_in, l0_v_norm, l0_w_s, l0_b_s, l0_w_out)
    x = _rglru_layer(x.reshape(BATCH, SEQ, D_MODEL), l1_norm, l1_w_in, l1_conv_w, l1_conv_b, l1_w_a, l1_b_a,
                     l1_w_x, l1_b_x, l1_lam, l1_w_out)
    x = _fox_layer(x, l2_norm, l2_w_in, l2_b_f, l2_q_norm, l2_k_norm, l2_w_out)
    x = _gmlp_layer(x.reshape(n, D_MODEL), l3_norm, l3_w_in, l3_v_norm, l3_w_s, l3_b_s, l3_w_out)
    return x.reshape(BATCH, SEQ, D_MODEL)
```

```python
import jax
import jax.numpy as jnp
from jax import lax
from jax.experimental import pallas as pl
from jax.experimental.pallas import tpu as pltpu

D_MODEL = 1024
BATCH = 8
SEQ = 2048
EPS = 1e-6
CHUNK = 128
E_A = 2 * D_MODEL
G_A = 8
DG_A = E_A // G_A
E_B = 3 * D_MODEL // 2
H_B = 16
BD_B = E_B // H_B
CONV_W = 4
LRU_C = 8.0
H_C = 16
DH_C = D_MODEL // H_C
E_C = H_C * DH_C

LANES = 128
VMEM_LIMIT = 56 * 1024 * 1024

TM_A = 512
T_B = 64
HG_B = 4
GW_B = HG_B * BD_B
TM_C = 512
TQ_C = 256
TK_C = 256
NH_C = 4
DEN_ROWS = 16
NEG_BIG = -1e30
LOG2E = 1.4426950408889634

F32 = jnp.float32
BF16 = jnp.bfloat16


def _dot(a, b):
    return jnp.dot(a, b, preferred_element_type=F32)


def _rms(x, g):
    ms = jnp.mean(x * x, axis=-1, keepdims=True)
    return x * lax.rsqrt(ms + EPS) * g


GELU_C = 0.7978845608028654


def _gelu_of_half(hx):
    return hx * (1.0 + jnp.tanh(hx * (2.0 * GELU_C + (8.0 * GELU_C * 0.044715) * (hx * hx))))


def _silu_of_half(hx):
    return hx * (1.0 + jnp.tanh(hx))


def _softplus(x):
    return jnp.maximum(x, 0.0) + jnp.log1p(jnp.exp(-jnp.abs(x)))


def _const_spec(shape):
    n = len(shape)
    return pl.BlockSpec(shape, lambda *_: (0,) * n, pipeline_mode=pl.Buffered(1))


def _gmlp_kernel(x_ref, nrm_ref, win_ref, vnrm_ref, ws_ref, bst_ref, wo_ref, o_ref, vn_scr, y_scr):
    x = x_ref[...]
    h = _rms(x, nrm_ref[...]).astype(BF16)
    v = _gelu_of_half(_dot(h, win_ref[:, E_A:2 * E_A]))
    vn_scr[...] = _rms(v, vnrm_ref[...]).astype(BF16)
    row = lax.broadcasted_iota(jnp.int32, (CHUNK, CHUNK), 0)
    col = lax.broadcasted_iota(jnp.int32, (CHUNK, CHUNK), 1)
    causal = col <= row
    for g in range(G_A):
        c0 = g * DG_A
        u = _gelu_of_half(_dot(h, win_ref[:, c0:c0 + DG_A]))
        gate = _silu_of_half(_dot(h, win_ref[:, 2 * E_A + c0:2 * E_A + c0 + DG_A]))
        w = jnp.where(causal, ws_ref[g], 0.0).astype(BF16)
        bias = bst_ref[:, g:g + 1]
        for c in range(TM_A // CHUNK):
            r0 = c * CHUNK
            mixed = _dot(w, vn_scr[r0:r0 + CHUNK, c0:c0 + DG_A]) + bias
            y = u[r0:r0 + CHUNK] * mixed * gate[r0:r0 + CHUNK]
            y_scr[r0:r0 + CHUNK, c0:c0 + DG_A] = y.astype(BF16)
    o_ref[...] = x + _dot(y_scr[...], wo_ref[...])


def _gmlp_layer(x2, norm, w_in, v_norm, w_s, b_s, w_out):
    n = x2.shape[0]
    return pl.pallas_call(
        _gmlp_kernel,
        grid=(n // TM_A,),
        in_specs=[
            pl.BlockSpec((TM_A, D_MODEL), lambda i: (i, 0)),
            _const_spec((1, D_MODEL)),
            _const_spec((D_MODEL, 3 * E_A)),
            _const_spec((1, E_A)),
            _const_spec((G_A, CHUNK, CHUNK)),
            _const_spec((CHUNK, G_A)),
            _const_spec((E_A, D_MODEL)),
        ],
        out_specs=pl.BlockSpec((TM_A, D_MODEL), lambda i: (i, 0)),
        out_shape=jax.ShapeDtypeStruct((n, D_MODEL), F32),
        scratch_shapes=[pltpu.VMEM((TM_A, E_A), BF16), pltpu.VMEM((TM_A, E_A), BF16)],
        compiler_params=pltpu.CompilerParams(dimension_semantics=("arbitrary",), vmem_limit_bytes=VMEM_LIMIT),
        name="gmlp_layer",
    )(x2, norm.reshape(1, D_MODEL), (0.5 * w_in).astype(BF16), v_norm.reshape(1, E_A), w_s, b_s.T,
      w_out.astype(BF16))


ROWS_B = T_B * BATCH
TAIL_B = (CONV_W - 1) * BATCH


def _rglru_kernel(x_ref, nrm_ref, win_ref, cw_ref, cb_ref, wbd_ref, hba_ref, hbx_ref, lam_ref, wo_ref, o_ref,
                  hn_scr, xb_scr, a_scr, hs_scr, state_scr, out_scr):
    @pl.when(pl.program_id(0) == 0)
    def _():
        xb_scr[0:TAIL_B, :] = jnp.zeros((TAIL_B, E_B), F32)
        state_scr[...] = jnp.zeros((BATCH, E_B), F32)

    nrm = nrm_ref[...]
    for b in range(BATCH):
        hb = _rms(x_ref[b], nrm)
        for c in range(D_MODEL // LANES):
            hn_scr[c, pl.ds(b, T_B, stride=BATCH), :] = hb[:, c * LANES:(c + 1) * LANES]
    h = jnp.concatenate([hn_scr[c] for c in range(D_MODEL // LANES)], axis=1).astype(BF16)
    xb_scr[TAIL_B:TAIL_B + ROWS_B, :] = _dot(h, win_ref[:, 0:E_B])
    gate = _silu_of_half(_dot(h, win_ref[:, E_B:2 * E_B]))

    xc = cb_ref[...] + cw_ref[0:1, :] * xb_scr[0:ROWS_B, :]
    for k in range(1, CONV_W):
        xc = xc + cw_ref[k:k + 1, :] * xb_scr[k * BATCH:k * BATCH + ROWS_B, :]
    tail = xb_scr[ROWS_B:ROWS_B + TAIL_B, :]
    xb_scr[0:TAIL_B, :] = tail
    xcb = xc.astype(BF16)

    quarter_c_sp = (0.25 * LRU_C) * _softplus(-lam_ref[...])
    for j in range(E_B // GW_B):
        c0 = j * GW_B
        pre = _dot(xcb[:, c0:c0 + GW_B], wbd_ref[j])
        ur = jnp.tanh(pre[:, 0:GW_B] + hba_ref[:, c0:c0 + GW_B])
        ui = jnp.tanh(pre[:, GW_B:2 * GW_B] + hbx_ref[:, c0:c0 + GW_B])
        tn = jnp.tanh(quarter_c_sp[:, c0:c0 + GW_B] * (1.0 + ur))
        d = 1.0 / (1.0 + tn)
        a_scr[:, c0:c0 + GW_B] = (1.0 - tn) * d
        root = jnp.where(tn <= 0.0, 0.0, tn * lax.rsqrt(tn))
        hs_scr[:, c0:c0 + GW_B] = (d * root) * ((1.0 + ui) * xc[:, c0:c0 + GW_B])

    hcur = state_scr[...]
    for t in range(T_B):
        rows = slice(t * BATCH, (t + 1) * BATCH)
        hcur = a_scr[rows, :] * hcur + hs_scr[rows, :]
        hs_scr[rows, :] = hcur
    state_scr[...] = hcur

    y = (hs_scr[...] * gate).astype(BF16)
    out = _dot(y, wo_ref[...])
    for c in range(D_MODEL // LANES):
        out_scr[c] = out[:, c * LANES:(c + 1) * LANES]
    for b in range(BATCH):
        ob = jnp.concatenate([out_scr[c, pl.ds(b, T_B, stride=BATCH), :] for c in range(D_MODEL // LANES)], axis=1)
        o_ref[b] = x_ref[b] + ob


def _block_diag_gates(w_a, w_x):
    def bd(w):
        wg = w.reshape(H_B // HG_B, HG_B, BD_B, BD_B)
        eye = jnp.eye(HG_B, dtype=w.dtype)
        return jnp.einsum('ghij,hk->ghikj', wg, eye).reshape(H_B // HG_B, GW_B, GW_B)
    return jnp.concatenate([bd(w_a), bd(w_x)], axis=-1)


def _rglru_layer(x3, norm, w_in, conv_w, conv_b, w_a, b_a, w_x, b_x, lam, w_out):
    ng = H_B // HG_B
    w_half = jnp.concatenate([w_in[:, 0:E_B], 0.5 * w_in[:, E_B:2 * E_B]], axis=1)
    return pl.pallas_call(
        _rglru_kernel,
        grid=(SEQ // T_B,),
        in_specs=[
            pl.BlockSpec((BATCH, T_B, D_MODEL), lambda i: (0, i, 0)),
            _const_spec((1, D_MODEL)),
            _const_spec((D_MODEL, 2 * E_B)),
            _const_spec((CONV_W, E_B)),
            _const_spec((1, E_B)),
            _const_spec((ng, GW_B, 2 * GW_B)),
            _const_spec((1, E_B)),
            _const_spec((1, E_B)),
            _const_spec((1, E_B)),
            _const_spec((E_B, D_MODEL)),
        ],
        out_specs=pl.BlockSpec((BATCH, T_B, D_MODEL), lambda i: (0, i, 0)),
        out_shape=jax.ShapeDtypeStruct((BATCH, SEQ, D_MODEL), F32),
        scratch_shapes=[
            pltpu.VMEM((D_MODEL // LANES, ROWS_B, LANES), F32),
            pltpu.VMEM((TAIL_B + ROWS_B, E_B), F32),
            pltpu.VMEM((ROWS_B, E_B), F32),
            pltpu.VMEM((ROWS_B, E_B), F32),
            pltpu.VMEM((BATCH, E_B), F32),
            pltpu.VMEM((D_MODEL // LANES, ROWS_B, LANES), F32),
        ],
        compiler_params=pltpu.CompilerParams(dimension_semantics=("arbitrary",), vmem_limit_bytes=VMEM_LIMIT),
        name="rglru_layer",
    )(x3, norm.reshape(1, D_MODEL), w_half.astype(BF16), conv_w, conv_b.reshape(1, E_B),
      (0.5 * _block_diag_gates(w_a, w_x)).astype(BF16), (0.5 * b_a).reshape(1, E_B), (0.5 * b_x).reshape(1, E_B),
      lam.reshape(1, E_B), w_out.astype(BF16))


N_SPLIT = 3


def _fox_proj_kernel(x_ref, nrm_ref, wkg_ref, wqvt_ref, wf_ref, bf_ref, kg_ref, seg_ref, segt_ref, place_ref,
                     qt_ref, k_ref, ek_ref, vt_ref, sg_ref, cum_ref, carry_scr):
    @pl.when(pl.program_id(1) == 0)
    def _():
        carry_scr[...] = jnp.zeros((1, LANES), F32)

    h = _rms(x_ref[0], nrm_ref[...]).astype(BF16)

    zk = _dot(h, wkg_ref[:, 0:E_C])
    ss = _dot((zk * zk).astype(BF16), seg_ref[...])
    r = lax.rsqrt(ss * (1.0 / DH_C) + EPS)
    r_hi = r.astype(BF16)
    r_lo = (r - r_hi.astype(F32)).astype(BF16)
    rexp = _dot(jnp.concatenate([r_hi, r_lo], axis=1), segt_ref[...])
    k_ref[0] = (zk * rexp * kg_ref[...]).astype(BF16)
    sg_ref[0] = _silu_of_half(_dot(h, wkg_ref[:, E_C:2 * E_C])).astype(BF16)

    zqv = lax.dot_general(wqvt_ref[...], h, (((1,), (1,)), ((), ())), preferred_element_type=F32)
    vt_ref[0] = zqv[E_C:2 * E_C].astype(BF16)
    for hd in range(H_C):
        rows = slice(hd * DH_C, (hd + 1) * DH_C)
        zq = zqv[rows]
        rq = lax.rsqrt(jnp.mean(zq * zq, axis=0, keepdims=True) + EPS)
        qt_ref[0, rows, :] = (zq * rq).astype(BF16)

    z = _dot(h, wf_ref[...]) + bf_ref[...]
    log_f = jnp.minimum(z, 0.0) - jnp.log1p(jnp.exp(-jnp.abs(z)))
    row = lax.broadcasted_iota(jnp.int32, (CHUNK, CHUNK), 0)
    col = lax.broadcasted_iota(jnp.int32, (CHUNK, CHUNK), 1)
    tri = jnp.where(col <= row, 1.0, 0.0).astype(F32)
    group = lax.broadcasted_iota(jnp.int32, (CHUNK, LANES), 1) // H_C
    carry = carry_scr[...]
    for c in range(TM_C // CHUNK):
        r0 = c * CHUNK
        cum = jnp.dot(tri, log_f[r0:r0 + CHUNK], preferred_element_type=F32,
                      precision=lax.Precision.HIGHEST) + carry
        carry = cum[CHUNK - 1:CHUNK, :]
        c2 = cum * LOG2E
        cum_ref[0, r0:r0 + CHUNK, :] = c2[:, 0:H_C]
        hi = c2.astype(BF16)
        r1 = c2 - hi.astype(F32)
        mid = r1.astype(BF16)
        lo = (r1 - mid.astype(F32)).astype(BF16)
        parts = jnp.where(group == 0, hi, jnp.where(group == 1, mid, lo))
        ek_ref[0, r0:r0 + CHUNK, :] = _dot(parts, place_ref[...]).astype(BF16)
    carry_scr[...] = carry


def _fox_proj(x3, norm, w_in, b_f, q_norm, k_norm):
    w_kg = jnp.concatenate([w_in[:, E_C:2 * E_C], 0.5 * w_in[:, 3 * E_C:4 * E_C]], axis=1).astype(BF16)
    w_qvt = jnp.concatenate([w_in[:, 0:E_C], w_in[:, 2 * E_C:3 * E_C]], axis=1).T.astype(BF16)
    pad = LANES - N_SPLIT * H_C
    w_f = jnp.pad(jnp.tile(w_in[:, 4 * E_C:], (1, N_SPLIT)), ((0, 0), (0, pad))).astype(BF16)
    b_fp = jnp.pad(jnp.tile(b_f, N_SPLIT), (0, pad)).reshape(1, LANES)
    kg = jnp.tile(k_norm * q_norm, H_C).reshape(1, E_C) * (DH_C ** -0.5 * LOG2E)
    head_of = jnp.arange(E_C) // DH_C
    seg = (head_of[:, None] == jnp.arange(LANES)[None, :]).astype(BF16)
    segt2 = jnp.concatenate([seg.T, seg.T], axis=0)
    src = jnp.arange(LANES)
    dst = (src % H_C) * DH_C + src // H_C
    place = jnp.where((src[:, None] < N_SPLIT * H_C) & (dst[:, None] == jnp.arange(E_C)[None, :]), -1.0, 0.0)
    tok = jax.ShapeDtypeStruct((BATCH, SEQ, E_C), BF16)
    tok_t = jax.ShapeDtypeStruct((BATCH, E_C, SEQ), BF16)
    blk = pl.BlockSpec((1, TM_C, E_C), lambda b, i: (b, i, 0))
    blk_t = pl.BlockSpec((1, E_C, TM_C), lambda b, i: (b, 0, i))
    return pl.pallas_call(
        _fox_proj_kernel,
        grid=(BATCH, SEQ // TM_C),
        in_specs=[
            pl.BlockSpec((1, TM_C, D_MODEL), lambda b, i: (b, i, 0)),
            _const_spec((1, D_MODEL)),
            _const_spec((D_MODEL, 2 * E_C)),
            _const_spec((2 * E_C, D_MODEL)),
            _const_spec((D_MODEL, LANES)),
            _const_spec((1, LANES)),
            _const_spec((1, E_C)),
            _const_spec((E_C, LANES)),
            _const_spec((2 * LANES, E_C)),
            _const_spec((LANES, E_C)),
        ],
        out_specs=[blk_t, blk, blk, blk_t, blk, pl.BlockSpec((1, TM_C, H_C), lambda b, i: (b, i, 0))],
        out_shape=[tok_t, tok, tok, tok_t, tok, jax.ShapeDtypeStruct((BATCH, SEQ, H_C), F32)],
        scratch_shapes=[pltpu.VMEM((1, LANES), F32)],
        compiler_params=pltpu.CompilerParams(dimension_semantics=("arbitrary", "arbitrary"),
                                             vmem_limit_bytes=VMEM_LIMIT),
        name="fox_proj",
    )(x3, norm.reshape(1, D_MODEL), w_kg, w_qvt, w_f, b_fp, kg, seg, segt2, place.astype(BF16))


def _attn_kernel(qt_ref, k_ref, ek_ref, vt_ref, sg_ref, cq_ref, o_ref,
                 qa_scr, acc_scr, m_scr, sta_scr, stb_scr, mxa_scr, mxb_scr):
    causal = (lax.broadcasted_iota(jnp.int32, (TK_C, TQ_C), 0) <= lax.broadcasted_iota(jnp.int32, (TK_C, TQ_C), 1))
    ones_v = jnp.ones((DEN_ROWS, TK_C), BF16)
    zero_q = jnp.zeros((DH_C, TQ_C), BF16)
    row_q = lax.broadcasted_iota(jnp.int32, (LANES, TQ_C), 0)
    heads = range(NH_C)

    def load_queries(qi):
        cols = slice(qi * TQ_C, (qi + 1) * TQ_C)
        for hh in heads:
            odd = hh % 2
            qt = qt_ref[0, hh * DH_C:(hh + 1) * DH_C, cols]
            qa_scr[hh, 0:LANES, :] = jnp.concatenate([zero_q, qt] if odd else [qt, zero_q], axis=0)
            mine = jnp.logical_and(row_q >= odd * DH_C, row_q < odd * DH_C + N_SPLIT)
            qa_scr[hh, LANES:2 * LANES, :] = jnp.where(mine, 1.0, 0.0).astype(BF16)

    def scores(qi, j, st_ref, mx_ref):
        keys = slice(j * TK_C, (j + 1) * TK_C)
        for hh in heads:
            pair = hh // 2
            ka = jnp.concatenate([k_ref[0, keys, pair * LANES:(pair + 1) * LANES],
                                  ek_ref[0, keys, pair * LANES:(pair + 1) * LANES]], axis=1)
            st = _dot(ka, qa_scr[hh])
            if j == qi:
                st = jnp.where(causal, st, NEG_BIG)
            st_ref[hh] = st
            mx_ref[hh] = jnp.max(st, axis=0, keepdims=True)

    def consume(qi, j, st_ref, mx_ref):
        alpha, pt = [], []
        for hh in heads:
            cq = cq_ref[0, 0, hh:hh + 1, qi * TQ_C:(qi + 1) * TQ_C]
            m_old = m_scr[hh]
            m = jnp.maximum(m_old, mx_ref[hh] + cq)
            pt.append(jnp.exp2(st_ref[hh] - (m - cq)).astype(BF16))
            alpha.append(jnp.exp2(m_old - m))
            m_scr[hh] = m
        pv = []
        for hh in heads:
            vt = vt_ref[0, hh * DH_C:(hh + 1) * DH_C, j * TK_C:(j + 1) * TK_C]
            pv.append(_dot(jnp.concatenate([vt, ones_v], axis=0), pt[hh]))
        for hh in heads:
            acc_scr[hh] = alpha[hh] * acc_scr[hh] + pv[hh]

    def finish(qi):
        rows = slice(qi * TQ_C, (qi + 1) * TQ_C)
        for pair in range(NH_C // 2):
            a0 = acc_scr[2 * pair]
            a1 = acc_scr[2 * pair + 1]
            ot = jnp.concatenate([a0[0:DH_C] / a0[DH_C:DH_C + 1], a1[0:DH_C] / a1[DH_C:DH_C + 1]], axis=0)
            cols = slice(pair * LANES, (pair + 1) * LANES)
            o_ref[0, rows, cols] = (ot.T * sg_ref[0, rows, cols].astype(F32)).astype(BF16)

    tiles = [(qi, j) for qi in range(SEQ // TQ_C) for j in range(qi + 1)]
    bufs = ((sta_scr, mxa_scr), (stb_scr, mxb_scr))
    def issue(n):
        qi, j = tiles[n]
        if j == 0:
            load_queries(qi)
        scores(qi, j, *bufs[n % 2])

    issue(0)
    issue(1)
    for n, (qi, j) in enumerate(tiles):
        if j == 0:
            acc_scr[...] = jnp.zeros(acc_scr.shape, F32)
            m_scr[...] = jnp.full(m_scr.shape, NEG_BIG, F32)
        consume(qi, j, *bufs[n % 2])
        if j == qi:
            finish(qi)
        if n + 2 < len(tiles):
            issue(n + 2)


def _fox_attention(qt, k, ek, vt, sg, cum):
    assert TQ_C == TK_C
    ng = H_C // NH_C
    w = NH_C * DH_C
    cq = cum.transpose(0, 2, 1).reshape(BATCH, ng, NH_C, SEQ)
    blk = pl.BlockSpec((1, SEQ, w), lambda b, g: (b, 0, g))
    blk_t = pl.BlockSpec((1, w, SEQ), lambda b, g: (b, g, 0))
    return pl.pallas_call(
        _attn_kernel,
        grid=(BATCH, ng),
        in_specs=[
            blk_t, blk, blk, blk_t, blk,
            pl.BlockSpec((1, 1, NH_C, SEQ), lambda b, g: (b, g, 0, 0)),
        ],
        out_specs=blk,
        out_shape=jax.ShapeDtypeStruct((BATCH, SEQ, E_C), BF16),
        scratch_shapes=[pltpu.VMEM((NH_C, 2 * LANES, TQ_C), BF16), pltpu.VMEM((NH_C, DH_C + DEN_ROWS, TQ_C), F32),
                        pltpu.VMEM((NH_C, 1, TQ_C), F32),
                        pltpu.VMEM((NH_C, TK_C, TQ_C), F32), pltpu.VMEM((NH_C, TK_C, TQ_C), F32),
                        pltpu.VMEM((NH_C, 1, TQ_C), F32), pltpu.VMEM((NH_C, 1, TQ_C), F32)],
        compiler_params=pltpu.CompilerParams(dimension_semantics=("arbitrary", "arbitrary"),
                                             vmem_limit_bytes=VMEM_LIMIT),
        name="fox_attention",
    )(qt, k, ek, vt, sg, cq)


def _out_proj_kernel(y_ref, x_ref, wo_ref, o_ref):
    o_ref[...] = x_ref[...] + _dot(y_ref[...], wo_ref[...])


def _fox_out_proj(y2, x2, w_out):
    n = x2.shape[0]
    return pl.pallas_call(
        _out_proj_kernel,
        grid=(n // TM_C,),
        in_specs=[
            pl.BlockSpec((TM_C, E_C), lambda i: (i, 0)),
            pl.BlockSpec((TM_C, D_MODEL), lambda i: (i, 0)),
            _const_spec((E_C, D_MODEL)),
        ],
        out_specs=pl.BlockSpec((TM_C, D_MODEL), lambda i: (i, 0)),
        out_shape=jax.ShapeDtypeStruct((n, D_MODEL), F32),
        compiler_params=pltpu.CompilerParams(dimension_semantics=("arbitrary",), vmem_limit_bytes=VMEM_LIMIT),
        name="fox_out_proj",
    )(y2, x2, w_out.astype(BF16))


def _fox_layer(x3, norm, w_in, b_f, q_norm, k_norm, w_out):
    qt, k, ek, vt, sg, cum = _fox_proj(x3, norm, w_in, b_f, q_norm, k_norm)
    y = _fox_attention(qt, k, ek, vt, sg, cum)
    n = BATCH * SEQ
    return _fox_out_proj(y.reshape(n, E_C), x3.reshape(n, D_MODEL), w_out).reshape(BATCH, SEQ, D_MODEL)


def kernel(x, l0_norm, l0_w_in, l0_v_norm, l0_w_s, l0_b_s, l0_w_out, l1_norm, l1_w_in, l1_conv_w, l1_conv_b, l1_w_a, l1_b_a, l1_w_x, l1_b_x, l1_lam, l1_w_out, l2_norm, l2_w_in, l2_b_f, l2_q_norm, l2_k_norm, l2_w_out, l3_norm, l3_w_in, l3_v_norm, l3_w_s, l3_b_s, l3_w_out):
    n = BATCH * SEQ
    x = _gmlp_layer(x.reshape(n, D_MODEL), l0_norm, l0_w_in, l0_v_norm, l0_w_s, l0_b_s, l0_w_out)
    x = _rglru_layer(x.reshape(BATCH, SEQ, D_MODEL), l1_norm, l1_w_in, l1_conv_w, l1_conv_b, l1_w_a, l1_b_a,
                     l1_w_x, l1_b_x, l1_lam, l1_w_out)
    x = _fox_layer(x, l2_norm, l2_w_in, l2_b_f, l2_q_norm, l2_k_norm, l2_w_out)
    x = _gmlp_layer(x.reshape(n, D_MODEL), l3_norm, l3_w_in, l3_v_norm, l3_w_s, l3_b_s, l3_w_out)
    return x.reshape(BATCH, SEQ, D_MODEL)
```

```python
import functools

import jax
import jax.numpy as jnp
from jax import lax
from jax.experimental import pallas as pl
from jax.experimental.pallas import tpu as pltpu

D_MODEL = 1024
BATCH = 8
SEQ = 2048
EPS = 1e-6
CHUNK = 128
E_A = 2 * D_MODEL
G_A = 8
DG_A = E_A // G_A
E_B = 3 * D_MODEL // 2
H_B = 16
BD_B = E_B // H_B
CONV_W = 4
LRU_C = 8.0
H_C = 16
DH_C = D_MODEL // H_C
E_C = H_C * DH_C

LANES = 128
VMEM_LIMIT = 56 * 1024 * 1024

TM_A = 512
T_B = 64
HG_B = 4
GW_B = HG_B * BD_B
TM_C = 512
TQ_C = 256
TK_C = 256
NH_C = 4
DEN_ROWS = 16
NEG_BIG = -1e30
LOG2E = 1.4426950408889634

F32 = jnp.float32
BF16 = jnp.bfloat16


def _dot(a, b):
    return jnp.dot(a, b, preferred_element_type=F32)


def _rms(x, g):
    ms = jnp.mean(x * x, axis=-1, keepdims=True)
    return x * lax.rsqrt(ms + EPS) * g


GELU_C = 0.7978845608028654


def _gelu_of_half(hx):
    return hx * (1.0 + jnp.tanh(hx * (2.0 * GELU_C + (8.0 * GELU_C * 0.044715) * (hx * hx))))


def _silu_of_half(hx):
    return hx * (1.0 + jnp.tanh(hx))


def _softplus(x):
    return jnp.maximum(x, 0.0) + jnp.log1p(jnp.exp(-jnp.abs(x)))


def _const_spec(shape):
    n = len(shape)
    return pl.BlockSpec(shape, lambda *_: (0,) * n, pipeline_mode=pl.Buffered(1))


def _gmlp_kernel(has_pending, x_ref, *refs):
    if has_pending:
        yp_ref, wp_ref, *refs = refs
        x = x_ref[...] + _dot(yp_ref[...], wp_ref[...])
    else:
        x = x_ref[...]
    nrm_ref, win_ref, vnrm_ref, ws_ref, bst_ref, wo_ref, o_ref, vn_scr, y_scr = refs
    h = _rms(x, nrm_ref[...]).astype(BF16)
    v = _gelu_of_half(_dot(h, win_ref[:, E_A:2 * E_A]))
    vn_scr[...] = _rms(v, vnrm_ref[...]).astype(BF16)
    row = lax.broadcasted_iota(jnp.int32, (CHUNK, CHUNK), 0)
    col = lax.broadcasted_iota(jnp.int32, (CHUNK, CHUNK), 1)
    causal = col <= row
    for g in range(G_A):
        c0 = g * DG_A
        u = _gelu_of_half(_dot(h, win_ref[:, c0:c0 + DG_A]))
        gate = _silu_of_half(_dot(h, win_ref[:, 2 * E_A + c0:2 * E_A + c0 + DG_A]))
        w = jnp.where(causal, ws_ref[g], 0.0).astype(BF16)
        bias = bst_ref[:, g:g + 1]
        for c in range(TM_A // CHUNK):
            r0 = c * CHUNK
            mixed = _dot(w, vn_scr[r0:r0 + CHUNK, c0:c0 + DG_A]) + bias
            y = u[r0:r0 + CHUNK] * mixed * gate[r0:r0 + CHUNK]
            y_scr[r0:r0 + CHUNK, c0:c0 + DG_A] = y.astype(BF16)
    o_ref[...] = x + _dot(y_scr[...], wo_ref[...])


def _gmlp_layer(x2, norm, w_in, v_norm, w_s, b_s, w_out, pending=None):
    n = x2.shape[0]
    rows = pl.BlockSpec((TM_A, D_MODEL), lambda i: (i, 0))
    pending_specs, pending_args = [], []
    if pending is not None:
        y_prev, w_prev = pending
        pending_specs = [pl.BlockSpec((TM_A, y_prev.shape[1]), lambda i: (i, 0)), _const_spec(w_prev.shape)]
        pending_args = [y_prev, w_prev.astype(BF16)]
    return pl.pallas_call(
        functools.partial(_gmlp_kernel, pending is not None),
        grid=(n // TM_A,),
        in_specs=[
            rows,
            *pending_specs,
            _const_spec((1, D_MODEL)),
            _const_spec((D_MODEL, 3 * E_A)),
            _const_spec((1, E_A)),
            _const_spec((G_A, CHUNK, CHUNK)),
            _const_spec((CHUNK, G_A)),
            _const_spec((E_A, D_MODEL)),
        ],
        out_specs=rows,
        out_shape=jax.ShapeDtypeStruct((n, D_MODEL), F32),
        scratch_shapes=[pltpu.VMEM((TM_A, E_A), BF16), pltpu.VMEM((TM_A, E_A), BF16)],
        compiler_params=pltpu.CompilerParams(dimension_semantics=("arbitrary",), vmem_limit_bytes=VMEM_LIMIT),
        name="gmlp_layer",
    )(x2, *pending_args, norm.reshape(1, D_MODEL), (0.5 * w_in).astype(BF16), v_norm.reshape(1, E_A), w_s, b_s.T,
      w_out.astype(BF16))


ROWS_B = T_B * BATCH
TAIL_B = (CONV_W - 1) * BATCH


def _rglru_kernel(x_ref, nrm_ref, win_ref, cw_ref, cb_ref, wbd_ref, hba_ref, hbx_ref, lam_ref, wo_ref, o_ref,
                  hn_scr, xb_scr, a_scr, hs_scr, state_scr, out_scr):
    @pl.when(pl.program_id(0) == 0)
    def _():
        xb_scr[0:TAIL_B, :] = jnp.zeros((TAIL_B, E_B), F32)
        state_scr[...] = jnp.zeros((BATCH, E_B), F32)

    nrm = nrm_ref[...]
    for b in range(BATCH):
        hb = _rms(x_ref[b], nrm)
        for c in range(D_MODEL // LANES):
            hn_scr[c, pl.ds(b, T_B, stride=BATCH), :] = hb[:, c * LANES:(c + 1) * LANES]
    h = jnp.concatenate([hn_scr[c] for c in range(D_MODEL // LANES)], axis=1).astype(BF16)
    xb_scr[TAIL_B:TAIL_B + ROWS_B, :] = _dot(h, win_ref[:, 0:E_B])
    gate = _silu_of_half(_dot(h, win_ref[:, E_B:2 * E_B]))

    xc = cb_ref[...] + cw_ref[0:1, :] * xb_scr[0:ROWS_B, :]
    for k in range(1, CONV_W):
        xc = xc + cw_ref[k:k + 1, :] * xb_scr[k * BATCH:k * BATCH + ROWS_B, :]
    tail = xb_scr[ROWS_B:ROWS_B + TAIL_B, :]
    xb_scr[0:TAIL_B, :] = tail
    xcb = xc.astype(BF16)

    quarter_c_sp = (0.25 * LRU_C) * _softplus(-lam_ref[...])
    for j in range(E_B // GW_B):
        c0 = j * GW_B
        pre = _dot(xcb[:, c0:c0 + GW_B], wbd_ref[j])
        ur = jnp.tanh(pre[:, 0:GW_B] + hba_ref[:, c0:c0 + GW_B])
        ui = jnp.tanh(pre[:, GW_B:2 * GW_B] + hbx_ref[:, c0:c0 + GW_B])
        tn = jnp.tanh(quarter_c_sp[:, c0:c0 + GW_B] * (1.0 + ur))
        d = 1.0 / (1.0 + tn)
        a_scr[:, c0:c0 + GW_B] = (1.0 - tn) * d
        root = jnp.where(tn <= 0.0, 0.0, tn * lax.rsqrt(tn))
        hs_scr[:, c0:c0 + GW_B] = (d * root) * ((1.0 + ui) * xc[:, c0:c0 + GW_B])

    hcur = state_scr[...]
    for t in range(T_B):
        rows = slice(t * BATCH, (t + 1) * BATCH)
        hcur = a_scr[rows, :] * hcur + hs_scr[rows, :]
        hs_scr[rows, :] = hcur
    state_scr[...] = hcur

    y = (hs_scr[...] * gate).astype(BF16)
    out = _dot(y, wo_ref[...])
    for c in range(D_MODEL // LANES):
        out_scr[c] = out[:, c * LANES:(c + 1) * LANES]
    for b in range(BATCH):
        ob = jnp.concatenate([out_scr[c, pl.ds(b, T_B, stride=BATCH), :] for c in range(D_MODEL // LANES)], axis=1)
        o_ref[b] = x_ref[b] + ob


def _block_diag_gates(w_a, w_x):
    def bd(w):
        wg = w.reshape(H_B // HG_B, HG_B, BD_B, BD_B)
        eye = jnp.eye(HG_B, dtype=w.dtype)
        return jnp.einsum('ghij,hk->ghikj', wg, eye).reshape(H_B // HG_B, GW_B, GW_B)
    return jnp.concatenate([bd(w_a), bd(w_x)], axis=-1)


def _rglru_layer(x3, norm, w_in, conv_w, conv_b, w_a, b_a, w_x, b_x, lam, w_out):
    ng = H_B // HG_B
    w_half = jnp.concatenate([w_in[:, 0:E_B], 0.5 * w_in[:, E_B:2 * E_B]], axis=1)
    return pl.pallas_call(
        _rglru_kernel,
        grid=(SEQ // T_B,),
        in_specs=[
            pl.BlockSpec((BATCH, T_B, D_MODEL), lambda i: (0, i, 0)),
            _const_spec((1, D_MODEL)),
            _const_spec((D_MODEL, 2 * E_B)),
            _const_spec((CONV_W, E_B)),
            _const_spec((1, E_B)),
            _const_spec((ng, GW_B, 2 * GW_B)),
            _const_spec((1, E_B)),
            _const_spec((1, E_B)),
            _const_spec((1, E_B)),
            _const_spec((E_B, D_MODEL)),
        ],
        out_specs=pl.BlockSpec((BATCH, T_B, D_MODEL), lambda i: (0, i, 0)),
        out_shape=jax.ShapeDtypeStruct((BATCH, SEQ, D_MODEL), F32),
        scratch_shapes=[
            pltpu.VMEM((D_MODEL // LANES, ROWS_B, LANES), F32),
            pltpu.VMEM((TAIL_B + ROWS_B, E_B), F32),
            pltpu.VMEM((ROWS_B, E_B), F32),
            pltpu.VMEM((ROWS_B, E_B), F32),
            pltpu.VMEM((BATCH, E_B), F32),
            pltpu.VMEM((D_MODEL // LANES, ROWS_B, LANES), F32),
        ],
        compiler_params=pltpu.CompilerParams(dimension_semantics=("arbitrary",), vmem_limit_bytes=VMEM_LIMIT),
        name="rglru_layer",
    )(x3, norm.reshape(1, D_MODEL), w_half.astype(BF16), conv_w, conv_b.reshape(1, E_B),
      (0.5 * _block_diag_gates(w_a, w_x)).astype(BF16), (0.5 * b_a).reshape(1, E_B), (0.5 * b_x).reshape(1, E_B),
      lam.reshape(1, E_B), w_out.astype(BF16))


N_SPLIT = 3


def _fox_proj_kernel(x_ref, nrm_ref, wkg_ref, wqvt_ref, wf_ref, bf_ref, kg_ref, seg_ref, segt_ref, place_ref, tri_ref,
                     qt_ref, k_ref, ek_ref, vt_ref, sg_ref, cum_ref, carry_scr):
    @pl.when(pl.program_id(1) == 0)
    def _():
        carry_scr[...] = jnp.zeros((1, LANES), F32)

    h = _rms(x_ref[0], nrm_ref[...]).astype(BF16)

    zk = _dot(h, wkg_ref[:, 0:E_C])
    ss = _dot((zk * zk).astype(BF16), seg_ref[...])
    r = lax.rsqrt(ss * (1.0 / DH_C) + EPS)
    r_hi = r.astype(BF16)
    r_lo = (r - r_hi.astype(F32)).astype(BF16)
    rexp = _dot(jnp.concatenate([r_hi, r_lo], axis=1), segt_ref[...])
    k_ref[0] = (zk * rexp * kg_ref[...]).astype(BF16)
    sg_ref[0] = _silu_of_half(_dot(h, wkg_ref[:, E_C:2 * E_C])).astype(BF16)

    zqv = lax.dot_general(wqvt_ref[...], h, (((1,), (1,)), ((), ())), preferred_element_type=F32)
    vt_ref[0] = zqv[E_C:2 * E_C].astype(BF16)
    for hd in range(H_C):
        rows = slice(hd * DH_C, (hd + 1) * DH_C)
        zq = zqv[rows]
        rq = lax.rsqrt(jnp.mean(zq * zq, axis=0, keepdims=True) + EPS)
        qt_ref[0, rows, :] = (zq * rq).astype(BF16)

    z = _dot(h, wf_ref[...]) + bf_ref[...]
    log_f = jnp.minimum(z, 0.0) - jnp.log1p(jnp.exp(-jnp.abs(z)))
    group = lax.broadcasted_iota(jnp.int32, (TM_C, LANES), 1) // H_C

    def split3(v):
        hi = v.astype(BF16).astype(F32)
        mid = (v - hi).astype(BF16).astype(F32)
        lo = v - hi - mid
        return jnp.where(group == 0, hi, jnp.where(group == 1, mid, lo)).astype(BF16)

    cp = _dot(tri_ref[...], split3(log_f))
    cum = cp + pltpu.roll(cp, LANES - H_C, axis=1) + pltpu.roll(cp, LANES - 2 * H_C, axis=1) + carry_scr[...]
    carry_scr[...] = cum[TM_C - 1:TM_C, :]
    c2 = cum * LOG2E
    cum_ref[0] = c2[:, 0:H_C]
    c2 = jnp.where(group == 0, c2,
                   jnp.where(group == 1, pltpu.roll(c2, H_C, axis=1), pltpu.roll(c2, 2 * H_C, axis=1)))
    ek_ref[0] = _dot(split3(c2), place_ref[...]).astype(BF16)


def _fox_proj(x3, norm, w_in, b_f, q_norm, k_norm):
    w_kg = jnp.concatenate([w_in[:, E_C:2 * E_C], 0.5 * w_in[:, 3 * E_C:4 * E_C]], axis=1).astype(BF16)
    w_qvt = jnp.concatenate([w_in[:, 0:E_C], w_in[:, 2 * E_C:3 * E_C]], axis=1).T.astype(BF16)
    pad = LANES - N_SPLIT * H_C
    w_f = jnp.pad(jnp.tile(w_in[:, 4 * E_C:], (1, N_SPLIT)), ((0, 0), (0, pad))).astype(BF16)
    b_fp = jnp.pad(jnp.tile(b_f, N_SPLIT), (0, pad)).reshape(1, LANES)
    kg = jnp.tile(k_norm * q_norm, H_C).reshape(1, E_C) * (DH_C ** -0.5 * LOG2E)
    head_of = jnp.arange(E_C) // DH_C
    seg = (head_of[:, None] == jnp.arange(LANES)[None, :]).astype(BF16)
    segt2 = jnp.concatenate([seg.T, seg.T], axis=0)
    src = jnp.arange(LANES)
    dst = (src % H_C) * DH_C + src // H_C
    place = jnp.where((src[:, None] < N_SPLIT * H_C) & (dst[:, None] == jnp.arange(E_C)[None, :]), -1.0, 0.0)
    tok = jax.ShapeDtypeStruct((BATCH, SEQ, E_C), BF16)
    tok_t = jax.ShapeDtypeStruct((BATCH, E_C, SEQ), BF16)
    blk = pl.BlockSpec((1, TM_C, E_C), lambda b, i: (b, i, 0))
    blk_t = pl.BlockSpec((1, E_C, TM_C), lambda b, i: (b, 0, i))
    return pl.pallas_call(
        _fox_proj_kernel,
        grid=(BATCH, SEQ // TM_C),
        in_specs=[
            pl.BlockSpec((1, TM_C, D_MODEL), lambda b, i: (b, i, 0)),
            _const_spec((1, D_MODEL)),
            _const_spec((D_MODEL, 2 * E_C)),
            _const_spec((2 * E_C, D_MODEL)),
            _const_spec((D_MODEL, LANES)),
            _const_spec((1, LANES)),
            _const_spec((1, E_C)),
            _const_spec((E_C, LANES)),
            _const_spec((2 * LANES, E_C)),
            _const_spec((LANES, E_C)),
            _const_spec((TM_C, TM_C)),
        ],
        out_specs=[blk_t, blk, blk, blk_t, blk, pl.BlockSpec((1, TM_C, H_C), lambda b, i: (b, i, 0))],
        out_shape=[tok_t, tok, tok, tok_t, tok, jax.ShapeDtypeStruct((BATCH, SEQ, H_C), F32)],
        scratch_shapes=[pltpu.VMEM((1, LANES), F32)],
        compiler_params=pltpu.CompilerParams(dimension_semantics=("arbitrary", "arbitrary"),
                                             vmem_limit_bytes=VMEM_LIMIT),
        name="fox_proj",
    )(x3, norm.reshape(1, D_MODEL), w_kg, w_qvt, w_f, b_fp, kg, seg, segt2, place.astype(BF16),
      jnp.tril(jnp.ones((TM_C, TM_C), BF16)))


def _attn_kernel(qt_ref, k_ref, ek_ref, vt_ref, sg_ref, cq_ref, o_ref,
                 qa_scr, acc_scr, m_scr, sta_scr, stb_scr, mxa_scr, mxb_scr):
    causal = (lax.broadcasted_iota(jnp.int32, (TK_C, TQ_C), 0) <= lax.broadcasted_iota(jnp.int32, (TK_C, TQ_C), 1))
    ones_v = jnp.ones((DEN_ROWS, TK_C), BF16)
    zero_q = jnp.zeros((DH_C, TQ_C), BF16)
    row_q = lax.broadcasted_iota(jnp.int32, (LANES, TQ_C), 0)
    heads = range(NH_C)

    def load_queries(qi):
        cols = slice(qi * TQ_C, (qi + 1) * TQ_C)
        for hh in heads:
            odd = hh % 2
            qt = qt_ref[0, hh * DH_C:(hh + 1) * DH_C, cols]
            qa_scr[hh, 0:LANES, :] = jnp.concatenate([zero_q, qt] if odd else [qt, zero_q], axis=0)
            mine = jnp.logical_and(row_q >= odd * DH_C, row_q < odd * DH_C + N_SPLIT)
            qa_scr[hh, LANES:2 * LANES, :] = jnp.where(mine, 1.0, 0.0).astype(BF16)

    def scores(qi, j, st_ref, mx_ref):
        keys = slice(j * TK_C, (j + 1) * TK_C)
        for hh in heads:
            pair = hh // 2
            ka = jnp.concatenate([k_ref[0, keys, pair * LANES:(pair + 1) * LANES],
                                  ek_ref[0, keys, pair * LANES:(pair + 1) * LANES]], axis=1)
            st = _dot(ka, qa_scr[hh])
            if j == qi:
                st = jnp.where(causal, st, NEG_BIG)
            st_ref[hh] = st
            mx_ref[hh] = jnp.max(st, axis=0, keepdims=True)

    def consume(qi, j, st_ref, mx_ref):
        alpha, pt = [], []
        for hh in heads:
            cq = cq_ref[0, 0, hh:hh + 1, qi * TQ_C:(qi + 1) * TQ_C]
            m_old = m_scr[hh]
            m = jnp.maximum(m_old, mx_ref[hh] + cq)
            pt.append(jnp.exp2(st_ref[hh] - (m - cq)).astype(BF16))
            alpha.append(jnp.exp2(m_old - m))
            m_scr[hh] = m
        pv = []
        for hh in heads:
            vt = vt_ref[0, hh * DH_C:(hh + 1) * DH_C, j * TK_C:(j + 1) * TK_C]
            pv.append(_dot(jnp.concatenate([vt, ones_v], axis=0), pt[hh]))
        for hh in heads:
            acc_scr[hh] = alpha[hh] * acc_scr[hh] + pv[hh]

    def finish(qi):
        rows = slice(qi * TQ_C, (qi + 1) * TQ_C)
        for pair in range(NH_C // 2):
            a0 = acc_scr[2 * pair]
            a1 = acc_scr[2 * pair + 1]
            ot = jnp.concatenate([a0[0:DH_C] / a0[DH_C:DH_C + 1], a1[0:DH_C] / a1[DH_C:DH_C + 1]], axis=0)
            cols = slice(pair * LANES, (pair + 1) * LANES)
            o_ref[0, rows, cols] = (ot.T * sg_ref[0, rows, cols].astype(F32)).astype(BF16)

    tiles = [(qi, j) for qi in range(SEQ // TQ_C) for j in range(qi + 1)]
    bufs = ((sta_scr, mxa_scr), (stb_scr, mxb_scr))
    def issue(n):
        qi, j = tiles[n]
        if j == 0:
            load_queries(qi)
        scores(qi, j, *bufs[n % 2])

    issue(0)
    issue(1)
    for n, (qi, j) in enumerate(tiles):
        if j == 0:
            acc_scr[...] = jnp.zeros(acc_scr.shape, F32)
            m_scr[...] = jnp.full(m_scr.shape, NEG_BIG, F32)
        consume(qi, j, *bufs[n % 2])
        if j == qi:
            finish(qi)
        if n + 2 < len(tiles):
            issue(n + 2)


def _fox_attention(qt, k, ek, vt, sg, cum):
    assert TQ_C == TK_C
    ng = H_C // NH_C
    w = NH_C * DH_C
    cq = cum.transpose(0, 2, 1).reshape(BATCH, ng, NH_C, SEQ)
    blk = pl.BlockSpec((1, SEQ, w), lambda b, g: (b, 0, g))
    blk_t = pl.BlockSpec((1, w, SEQ), lambda b, g: (b, g, 0))
    return pl.pallas_call(
        _attn_kernel,
        grid=(BATCH, ng),
        in_specs=[
            blk_t, blk, blk, blk_t, blk,
            pl.BlockSpec((1, 1, NH_C, SEQ), lambda b, g: (b, g, 0, 0)),
        ],
        out_specs=blk,
        out_shape=jax.ShapeDtypeStruct((BATCH, SEQ, E_C), BF16),
        scratch_shapes=[pltpu.VMEM((NH_C, 2 * LANES, TQ_C), BF16), pltpu.VMEM((NH_C, DH_C + DEN_ROWS, TQ_C), F32),
                        pltpu.VMEM((NH_C, 1, TQ_C), F32),
                        pltpu.VMEM((NH_C, TK_C, TQ_C), F32), pltpu.VMEM((NH_C, TK_C, TQ_C), F32),
                        pltpu.VMEM((NH_C, 1, TQ_C), F32), pltpu.VMEM((NH_C, 1, TQ_C), F32)],
        compiler_params=pltpu.CompilerParams(dimension_semantics=("arbitrary", "arbitrary"),
                                             vmem_limit_bytes=VMEM_LIMIT),
        name="fox_attention",
    )(qt, k, ek, vt, sg, cq)


def _fox_mixer(x3, norm, w_in, b_f, q_norm, k_norm):
    qt, k, ek, vt, sg, cum = _fox_proj(x3, norm, w_in, b_f, q_norm, k_norm)
    return _fox_attention(qt, k, ek, vt, sg, cum)


def kernel(x, l0_norm, l0_w_in, l0_v_norm, l0_w_s, l0_b_s, l0_w_out, l1_norm, l1_w_in, l1_conv_w, l1_conv_b, l1_w_a, l1_b_a, l1_w_x, l1_b_x, l1_lam, l1_w_out, l2_norm, l2_w_in, l2_b_f, l2_q_norm, l2_k_norm, l2_w_out, l3_norm, l3_w_in, l3_v_norm, l3_w_s, l3_b_s, l3_w_out):
    n = BATCH * SEQ
    x = _gmlp_layer(x.reshape(n, D_MODEL), l0_norm, l0_w_in, l0_v_norm, l0_w_s, l0_b_s, l0_w_out)
    x = _rglru_layer(x.reshape(BATCH, SEQ, D_MODEL), l1_norm, l1_w_in, l1_conv_w, l1_conv_b, l1_w_a, l1_b_a,
                     l1_w_x, l1_b_x, l1_lam, l1_w_out)
    y = _fox_mixer(x, l2_norm, l2_w_in, l2_b_f, l2_q_norm, l2_k_norm)
    x = _gmlp_layer(x.reshape(n, D_MODEL), l3_norm, l3_w_in, l3_v_norm, l3_w_s, l3_b_s, l3_w_out,
                    pending=(y.reshape(n, E_C), l2_w_out))
    return x.reshape(BATCH, SEQ, D_MODEL)
```

```python
import functools

import jax
import jax.numpy as jnp
from jax import lax
from jax.experimental import pallas as pl
from jax.experimental.pallas import tpu as pltpu

D_MODEL = 1024
BATCH = 8
SEQ = 2048
EPS = 1e-6
CHUNK = 128
E_A = 2 * D_MODEL
G_A = 8
DG_A = E_A // G_A
E_B = 3 * D_MODEL // 2
H_B = 16
BD_B = E_B // H_B
CONV_W = 4
LRU_C = 8.0
H_C = 16
DH_C = D_MODEL // H_C
E_C = H_C * DH_C

LANES = 128
VMEM_LIMIT = 56 * 1024 * 1024

TM_A = 512
T_B = 64
HG_B = 4
GW_B = HG_B * BD_B
TM_C = 512
TQ_C = 256
TK_C = 256
NH_C = 4
DEN_ROWS = 16
NEG_BIG = -1e30
LOG2E = 1.4426950408889634

F32 = jnp.float32
BF16 = jnp.bfloat16


def _dot(a, b):
    return jnp.dot(a, b, preferred_element_type=F32)


def _rms(x, g):
    ms = jnp.mean(x * x, axis=-1, keepdims=True)
    return x * lax.rsqrt(ms + EPS) * g


GELU_C = 0.7978845608028654


def _gelu_of_half(hx):
    return hx * (1.0 + jnp.tanh(hx * (2.0 * GELU_C + (8.0 * GELU_C * 0.044715) * (hx * hx))))


def _silu_of_half(hx):
    return hx * (1.0 + jnp.tanh(hx))


def _softplus(x):
    return jnp.maximum(x, 0.0) + jnp.log1p(jnp.exp(-jnp.abs(x)))


def _const_spec(shape):
    n = len(shape)
    return pl.BlockSpec(shape, lambda *_: (0,) * n, pipeline_mode=pl.Buffered(1))


def _gmlp_kernel(has_pending, x_ref, *refs):
    if has_pending:
        yp_ref, wp_ref, *refs = refs
        x = x_ref[...] + _dot(yp_ref[...], wp_ref[...])
    else:
        x = x_ref[...]
    nrm_ref, win_ref, vnrm_ref, ws_ref, bst_ref, wo_ref, o_ref, vn_scr, y_scr = refs
    h = _rms(x, nrm_ref[...]).astype(BF16)
    v = _gelu_of_half(_dot(h, win_ref[:, E_A:2 * E_A]))
    vn_scr[...] = _rms(v, vnrm_ref[...]).astype(BF16)
    row = lax.broadcasted_iota(jnp.int32, (CHUNK, CHUNK), 0)
    col = lax.broadcasted_iota(jnp.int32, (CHUNK, CHUNK), 1)
    causal = col <= row
    for g in range(G_A):
        c0 = g * DG_A
        u = _gelu_of_half(_dot(h, win_ref[:, c0:c0 + DG_A]))
        gate = _silu_of_half(_dot(h, win_ref[:, 2 * E_A + c0:2 * E_A + c0 + DG_A]))
        w = jnp.where(causal, ws_ref[g], 0.0).astype(BF16)
        bias = bst_ref[:, g:g + 1]
        for c in range(TM_A // CHUNK):
            r0 = c * CHUNK
            mixed = _dot(w, vn_scr[r0:r0 + CHUNK, c0:c0 + DG_A]) + bias
            y = u[r0:r0 + CHUNK] * mixed * gate[r0:r0 + CHUNK]
            y_scr[r0:r0 + CHUNK, c0:c0 + DG_A] = y.astype(BF16)
    o_ref[...] = x + _dot(y_scr[...], wo_ref[...])


def _gmlp_layer(x2, norm, w_in, v_norm, w_s, b_s, w_out, pending=None):
    n = x2.shape[0]
    rows = pl.BlockSpec((TM_A, D_MODEL), lambda i: (i, 0))
    pending_specs, pending_args = [], []
    if pending is not None:
        y_prev, w_prev = pending
        pending_specs = [pl.BlockSpec((TM_A, y_prev.shape[1]), lambda i: (i, 0)), _const_spec(w_prev.shape)]
        pending_args = [y_prev, w_prev.astype(BF16)]
    return pl.pallas_call(
        functools.partial(_gmlp_kernel, pending is not None),
        grid=(n // TM_A,),
        in_specs=[
            rows,
            *pending_specs,
            _const_spec((1, D_MODEL)),
            _const_spec((D_MODEL, 3 * E_A)),
            _const_spec((1, E_A)),
            _const_spec((G_A, CHUNK, CHUNK)),
            _const_spec((CHUNK, G_A)),
            _const_spec((E_A, D_MODEL)),
        ],
        out_specs=rows,
        out_shape=jax.ShapeDtypeStruct((n, D_MODEL), F32),
        scratch_shapes=[pltpu.VMEM((TM_A, E_A), BF16), pltpu.VMEM((TM_A, E_A), BF16)],
        compiler_params=pltpu.CompilerParams(dimension_semantics=("arbitrary",), vmem_limit_bytes=VMEM_LIMIT),
        name="gmlp_layer",
    )(x2, *pending_args, norm.reshape(1, D_MODEL), (0.5 * w_in).astype(BF16), v_norm.reshape(1, E_A), w_s, b_s.T,
      w_out.astype(BF16))


ROWS_B = T_B * BATCH
TAIL_B = (CONV_W - 1) * BATCH


def _rglru_kernel(x_ref, nrm_ref, win_ref, cw_ref, cb_ref, wbd_ref, hba_ref, hbx_ref, lam_ref, wo_ref, o_ref,
                  hn_scr, xb_scr, a_scr, hs_scr, state_scr, out_scr):
    @pl.when(pl.program_id(0) == 0)
    def _():
        xb_scr[0:TAIL_B, :] = jnp.zeros((TAIL_B, E_B), F32)
        state_scr[...] = jnp.zeros((BATCH, E_B), F32)

    nrm = nrm_ref[...]
    for b in range(BATCH):
        hb = _rms(x_ref[b], nrm)
        for c in range(D_MODEL // LANES):
            hn_scr[c, pl.ds(b, T_B, stride=BATCH), :] = hb[:, c * LANES:(c + 1) * LANES]
    h = jnp.concatenate([hn_scr[c] for c in range(D_MODEL // LANES)], axis=1).astype(BF16)
    xb_scr[TAIL_B:TAIL_B + ROWS_B, :] = _dot(h, win_ref[:, 0:E_B])
    gate = _silu_of_half(_dot(h, win_ref[:, E_B:2 * E_B]))

    xc = cb_ref[...] + cw_ref[0:1, :] * xb_scr[0:ROWS_B, :]
    for k in range(1, CONV_W):
        xc = xc + cw_ref[k:k + 1, :] * xb_scr[k * BATCH:k * BATCH + ROWS_B, :]
    tail = xb_scr[ROWS_B:ROWS_B + TAIL_B, :]
    xb_scr[0:TAIL_B, :] = tail
    xcb = xc.astype(BF16)

    quarter_c_sp = (0.25 * LRU_C) * _softplus(-lam_ref[...])
    for j in range(E_B // GW_B):
        c0 = j * GW_B
        pre = _dot(xcb[:, c0:c0 + GW_B], wbd_ref[j])
        ur = jnp.tanh(pre[:, 0:GW_B] + hba_ref[:, c0:c0 + GW_B])
        ui = jnp.tanh(pre[:, GW_B:2 * GW_B] + hbx_ref[:, c0:c0 + GW_B])
        tn = jnp.tanh(quarter_c_sp[:, c0:c0 + GW_B] * (1.0 + ur))
        d = 1.0 / (1.0 + tn)
        a_scr[:, c0:c0 + GW_B] = (1.0 - tn) * d
        root = jnp.where(tn <= 0.0, 0.0, tn * lax.rsqrt(tn))
        hs_scr[:, c0:c0 + GW_B] = (d * root) * ((1.0 + ui) * xc[:, c0:c0 + GW_B])

    hcur = state_scr[...]
    for t in range(T_B):
        rows = slice(t * BATCH, (t + 1) * BATCH)
        hcur = a_scr[rows, :] * hcur + hs_scr[rows, :]
        hs_scr[rows, :] = hcur
    state_scr[...] = hcur

    y = (hs_scr[...] * gate).astype(BF16)
    out = _dot(y, wo_ref[...])
    for c in range(D_MODEL // LANES):
        out_scr[c] = out[:, c * LANES:(c + 1) * LANES]
    for b in range(BATCH):
        ob = jnp.concatenate([out_scr[c, pl.ds(b, T_B, stride=BATCH), :] for c in range(D_MODEL // LANES)], axis=1)
        o_ref[b] = x_ref[b] + ob


def _block_diag_gates(w_a, w_x):
    def bd(w):
        wg = w.reshape(H_B // HG_B, HG_B, BD_B, BD_B)
        eye = jnp.eye(HG_B, dtype=w.dtype)
        return jnp.einsum('ghij,hk->ghikj', wg, eye).reshape(H_B // HG_B, GW_B, GW_B)
    return jnp.concatenate([bd(w_a), bd(w_x)], axis=-1)


def _rglru_layer(x3, norm, w_in, conv_w, conv_b, w_a, b_a, w_x, b_x, lam, w_out):
    ng = H_B // HG_B
    w_half = jnp.concatenate([w_in[:, 0:E_B], 0.5 * w_in[:, E_B:2 * E_B]], axis=1)
    return pl.pallas_call(
        _rglru_kernel,
        grid=(SEQ // T_B,),
        in_specs=[
            pl.BlockSpec((BATCH, T_B, D_MODEL), lambda i: (0, i, 0)),
            _const_spec((1, D_MODEL)),
            _const_spec((D_MODEL, 2 * E_B)),
            _const_spec((CONV_W, E_B)),
            _const_spec((1, E_B)),
            _const_spec((ng, GW_B, 2 * GW_B)),
            _const_spec((1, E_B)),
            _const_spec((1, E_B)),
            _const_spec((1, E_B)),
            _const_spec((E_B, D_MODEL)),
        ],
        out_specs=pl.BlockSpec((BATCH, T_B, D_MODEL), lambda i: (0, i, 0)),
        out_shape=jax.ShapeDtypeStruct((BATCH, SEQ, D_MODEL), F32),
        scratch_shapes=[
            pltpu.VMEM((D_MODEL // LANES, ROWS_B, LANES), F32),
            pltpu.VMEM((TAIL_B + ROWS_B, E_B), F32),
            pltpu.VMEM((ROWS_B, E_B), F32),
            pltpu.VMEM((ROWS_B, E_B), F32),
            pltpu.VMEM((BATCH, E_B), F32),
            pltpu.VMEM((D_MODEL // LANES, ROWS_B, LANES), F32),
        ],
        compiler_params=pltpu.CompilerParams(dimension_semantics=("arbitrary",), vmem_limit_bytes=VMEM_LIMIT),
        name="rglru_layer",
    )(x3, norm.reshape(1, D_MODEL), w_half.astype(BF16), conv_w, conv_b.reshape(1, E_B),
      (0.5 * _block_diag_gates(w_a, w_x)).astype(BF16), (0.5 * b_a).reshape(1, E_B), (0.5 * b_x).reshape(1, E_B),
      lam.reshape(1, E_B), w_out.astype(BF16))


N_SPLIT = 3


def _fox_proj_kernel(x_ref, nrm_ref, wkg_ref, wqvt_ref, wf_ref, bf_ref, kg_ref, seg_ref, segt_ref, place_ref, tri_ref,
                     qt_ref, k_ref, ek_ref, vt_ref, sg_ref, cum_ref, carry_scr):
    @pl.when(pl.program_id(1) == 0)
    def _():
        carry_scr[...] = jnp.zeros((1, LANES), F32)

    h = _rms(x_ref[0], nrm_ref[...]).astype(BF16)

    zk = _dot(h, wkg_ref[:, 0:E_C])
    ss = _dot((zk * zk).astype(BF16), seg_ref[...])
    r = lax.rsqrt(ss * (1.0 / DH_C) + EPS)
    r_hi = r.astype(BF16)
    r_lo = (r - r_hi.astype(F32)).astype(BF16)
    rexp = _dot(jnp.concatenate([r_hi, r_lo], axis=1), segt_ref[...])
    k_ref[0] = (zk * rexp * kg_ref[...]).astype(BF16)
    sg_ref[0] = _silu_of_half(_dot(h, wkg_ref[:, E_C:2 * E_C])).astype(BF16)

    zqv = lax.dot_general(wqvt_ref[...], h, (((1,), (1,)), ((), ())), preferred_element_type=F32)
    vt_ref[0] = zqv[E_C:2 * E_C].astype(BF16)
    for hd in range(H_C):
        rows = slice(hd * DH_C, (hd + 1) * DH_C)
        zq = zqv[rows]
        rq = lax.rsqrt(jnp.mean(zq * zq, axis=0, keepdims=True) + EPS)
        qt_ref[0, rows, :] = (zq * rq).astype(BF16)

    z = _dot(h, wf_ref[...]) + bf_ref[...]
    log_f = jnp.minimum(z, 0.0) - jnp.log1p(jnp.exp(-jnp.abs(z)))
    group = lax.broadcasted_iota(jnp.int32, (TM_C, LANES), 1) // H_C

    def split3(v):
        hi = v.astype(BF16).astype(F32)
        mid = (v - hi).astype(BF16).astype(F32)
        lo = v - hi - mid
        return jnp.where(group == 0, hi, jnp.where(group == 1, mid, lo)).astype(BF16)

    cp = _dot(tri_ref[...], split3(log_f))
    cum = cp + pltpu.roll(cp, LANES - H_C, axis=1) + pltpu.roll(cp, LANES - 2 * H_C, axis=1) + carry_scr[...]
    carry_scr[...] = cum[TM_C - 1:TM_C, :]
    c2 = cum * LOG2E
    cum_ref[0] = c2[:, 0:H_C]
    c2 = jnp.where(group == 0, c2,
                   jnp.where(group == 1, pltpu.roll(c2, H_C, axis=1), pltpu.roll(c2, 2 * H_C, axis=1)))
    ek_ref[0] = _dot(split3(c2), place_ref[...]).astype(BF16)


def _fox_proj(x3, norm, w_in, b_f, q_norm, k_norm):
    w_kg = jnp.concatenate([w_in[:, E_C:2 * E_C], 0.5 * w_in[:, 3 * E_C:4 * E_C]], axis=1).astype(BF16)
    w_qvt = jnp.concatenate([w_in[:, 0:E_C], w_in[:, 2 * E_C:3 * E_C]], axis=1).T.astype(BF16)
    pad = LANES - N_SPLIT * H_C
    w_f = jnp.pad(jnp.tile(w_in[:, 4 * E_C:], (1, N_SPLIT)), ((0, 0), (0, pad))).astype(BF16)
    b_fp = jnp.pad(jnp.tile(b_f, N_SPLIT), (0, pad)).reshape(1, LANES)
    kg = jnp.tile(k_norm * q_norm, H_C).reshape(1, E_C) * (DH_C ** -0.5 * LOG2E)
    head_of = jnp.arange(E_C) // DH_C
    seg = (head_of[:, None] == jnp.arange(LANES)[None, :]).astype(BF16)
    segt2 = jnp.concatenate([seg.T, seg.T], axis=0)
    src = jnp.arange(LANES)
    dst = (src % H_C) * DH_C + src // H_C
    place = jnp.where((src[:, None] < N_SPLIT * H_C) & (dst[:, None] == jnp.arange(E_C)[None, :]), -1.0, 0.0)
    tok = jax.ShapeDtypeStruct((BATCH, SEQ, E_C), BF16)
    tok_t = jax.ShapeDtypeStruct((BATCH, E_C, SEQ), BF16)
    blk = pl.BlockSpec((1, TM_C, E_C), lambda b, i: (b, i, 0))
    blk_t = pl.BlockSpec((1, E_C, TM_C), lambda b, i: (b, 0, i))
    return pl.pallas_call(
        _fox_proj_kernel,
        grid=(BATCH, SEQ // TM_C),
        in_specs=[
            pl.BlockSpec((1, TM_C, D_MODEL), lambda b, i: (b, i, 0)),
            _const_spec((1, D_MODEL)),
            _const_spec((D_MODEL, 2 * E_C)),
            _const_spec((2 * E_C, D_MODEL)),
            _const_spec((D_MODEL, LANES)),
            _const_spec((1, LANES)),
            _const_spec((1, E_C)),
            _const_spec((E_C, LANES)),
            _const_spec((2 * LANES, E_C)),
            _const_spec((LANES, E_C)),
            _const_spec((TM_C, TM_C)),
        ],
        out_specs=[blk_t, blk, blk, blk_t, blk, pl.BlockSpec((1, TM_C, H_C), lambda b, i: (b, i, 0))],
        out_shape=[tok_t, tok, tok, tok_t, tok, jax.ShapeDtypeStruct((BATCH, SEQ, H_C), F32)],
        scratch_shapes=[pltpu.VMEM((1, LANES), F32)],
        compiler_params=pltpu.CompilerParams(dimension_semantics=("arbitrary", "arbitrary"),
                                             vmem_limit_bytes=VMEM_LIMIT),
        name="fox_proj",
    )(x3, norm.reshape(1, D_MODEL), w_kg, w_qvt, w_f, b_fp, kg, seg, segt2, place.astype(BF16),
      jnp.tril(jnp.ones((TM_C, TM_C), BF16)))


def _attn_kernel(qt_ref, k_ref, ek_ref, vt_ref, sg_ref, cq_ref, o_ref,
                 qa_scr, acc_scr, m_scr, sta_scr, stb_scr, mxa_scr, mxb_scr):
    causal = (lax.broadcasted_iota(jnp.int32, (TK_C, TQ_C), 0) <= lax.broadcasted_iota(jnp.int32, (TK_C, TQ_C), 1))
    ones_v = jnp.ones((DEN_ROWS, TK_C), BF16)
    zero_q = jnp.zeros((DH_C, TQ_C), BF16)
    row_q = lax.broadcasted_iota(jnp.int32, (LANES, TQ_C), 0)
    heads = range(NH_C)

    def load_queries(qi):
        cols = slice(qi * TQ_C, (qi + 1) * TQ_C)
        for hh in heads:
            odd = hh % 2
            qt = qt_ref[0, hh * DH_C:(hh + 1) * DH_C, cols]
            qa_scr[hh, 0:LANES, :] = jnp.concatenate([zero_q, qt] if odd else [qt, zero_q], axis=0)
            mine = jnp.logical_and(row_q >= odd * DH_C, row_q < odd * DH_C + N_SPLIT)
            qa_scr[hh, LANES:2 * LANES, :] = jnp.where(mine, 1.0, 0.0).astype(BF16)

    def scores(qi, j, st_ref, mx_ref):
        keys = slice(j * TK_C, (j + 1) * TK_C)
        for hh in heads:
            pair = hh // 2
            ka = jnp.concatenate([k_ref[0, keys, pair * LANES:(pair + 1) * LANES],
                                  ek_ref[0, keys, pair * LANES:(pair + 1) * LANES]], axis=1)
            st = _dot(ka, qa_scr[hh])
            if j == qi:
                st = jnp.where(causal, st, NEG_BIG)
            st_ref[hh] = st
            mx_ref[hh] = jnp.max(st, axis=0, keepdims=True)

    def consume(qi, j, st_ref, mx_ref):
        alpha, pt = [], []
        for hh in heads:
            cq = cq_ref[0, 0, hh:hh + 1, qi * TQ_C:(qi + 1) * TQ_C]
            m_old = m_scr[hh]
            m = jnp.maximum(m_old, mx_ref[hh] + cq)
            pt.append(jnp.exp2((st_ref[hh] - (m - cq)).astype(BF16)))
            alpha.append(jnp.exp2(m_old - m))
            m_scr[hh] = m
        pv = []
        for hh in heads:
            vt = vt_ref[0, hh * DH_C:(hh + 1) * DH_C, j * TK_C:(j + 1) * TK_C]
            pv.append(_dot(jnp.concatenate([vt, ones_v], axis=0), pt[hh]))
        for hh in heads:
            acc_scr[hh] = alpha[hh] * acc_scr[hh] + pv[hh]

    def finish(qi):
        rows = slice(qi * TQ_C, (qi + 1) * TQ_C)
        for pair in range(NH_C // 2):
            a0 = acc_scr[2 * pair]
            a1 = acc_scr[2 * pair + 1]
            ot = jnp.concatenate([a0[0:DH_C] / a0[DH_C:DH_C + 1], a1[0:DH_C] / a1[DH_C:DH_C + 1]], axis=0)
            cols = slice(pair * LANES, (pair + 1) * LANES)
            o_ref[0, rows, cols] = (ot.T * sg_ref[0, rows, cols].astype(F32)).astype(BF16)

    tiles = [(qi, j) for qi in range(SEQ // TQ_C) for j in range(qi + 1)]
    bufs = ((sta_scr, mxa_scr), (stb_scr, mxb_scr))
    def issue(n):
        qi, j = tiles[n]
        if j == 0:
            load_queries(qi)
        scores(qi, j, *bufs[n % 2])

    issue(0)
    issue(1)
    for n, (qi, j) in enumerate(tiles):
        if j == 0:
            acc_scr[...] = jnp.zeros(acc_scr.shape, F32)
            m_scr[...] = jnp.full(m_scr.shape, NEG_BIG, F32)
        consume(qi, j, *bufs[n % 2])
        if j == qi:
            finish(qi)
        if n + 2 < len(tiles):
            issue(n + 2)


def _fox_attention(qt, k, ek, vt, sg, cum):
    assert TQ_C == TK_C
    ng = H_C // NH_C
    w = NH_C * DH_C
    cq = cum.transpose(0, 2, 1).reshape(BATCH, ng, NH_C, SEQ)
    blk = pl.BlockSpec((1, SEQ, w), lambda b, g: (b, 0, g))
    blk_t = pl.BlockSpec((1, w, SEQ), lambda b, g: (b, g, 0))
    return pl.pallas_call(
        _attn_kernel,
        grid=(BATCH, ng),
        in_specs=[
            blk_t, blk, blk, blk_t, blk,
            pl.BlockSpec((1, 1, NH_C, SEQ), lambda b, g: (b, g, 0, 0)),
        ],
        out_specs=blk,
        out_shape=jax.ShapeDtypeStruct((BATCH, SEQ, E_C), BF16),
        scratch_shapes=[pltpu.VMEM((NH_C, 2 * LANES, TQ_C), BF16), pltpu.VMEM((NH_C, DH_C + DEN_ROWS, TQ_C), F32),
                        pltpu.VMEM((NH_C, 1, TQ_C), F32),
                        pltpu.VMEM((NH_C, TK_C, TQ_C), F32), pltpu.VMEM((NH_C, TK_C, TQ_C), F32),
                        pltpu.VMEM((NH_C, 1, TQ_C), F32), pltpu.VMEM((NH_C, 1, TQ_C), F32)],
        compiler_params=pltpu.CompilerParams(dimension_semantics=("arbitrary", "arbitrary"),
                                             vmem_limit_bytes=VMEM_LIMIT),
        name="fox_attention",
    )(qt, k, ek, vt, sg, cq)


def _fox_mixer(x3, norm, w_in, b_f, q_norm, k_norm):
    qt, k, ek, vt, sg, cum = _fox_proj(x3, norm, w_in, b_f, q_norm, k_norm)
    return _fox_attention(qt, k, ek, vt, sg, cum)


def kernel(x, l0_norm, l0_w_in, l0_v_norm, l0_w_s, l0_b_s, l0_w_out, l1_norm, l1_w_in, l1_conv_w, l1_conv_b, l1_w_a, l1_b_a, l1_w_x, l1_b_x, l1_lam, l1_w_out, l2_norm, l2_w_in, l2_b_f, l2_q_norm, l2_k_norm, l2_w_out, l3_norm, l3_w_in, l3_v_norm, l3_w_s, l3_b_s, l3_w_out):
    n = BATCH * SEQ
    x = _gmlp_layer(x.reshape(n, D_MODEL), l0_norm, l0_w_in, l0_v_norm, l0_w_s, l0_b_s, l0_w_out)
    x = _rglru_layer(x.reshape(BATCH, SEQ, D_MODEL), l1_norm, l1_w_in, l1_conv_w, l1_conv_b, l1_w_a, l1_b_a,
                     l1_w_x, l1_b_x, l1_lam, l1_w_out)
    y = _fox_mixer(x, l2_norm, l2_w_in, l2_b_f, l2_q_norm, l2_k_norm)
    x = _gmlp_layer(x.reshape(n, D_MODEL), l3_norm, l3_w_in, l3_v_norm, l3_w_s, l3_b_s, l3_w_out,
                    pending=(y.reshape(n, E_C), l2_w_out))
    return x.reshape(BATCH, SEQ, D_MODEL)
```

```python
import functools

import jax
import jax.numpy as jnp
from jax import lax
from jax.experimental import pallas as pl
from jax.experimental.pallas import tpu as pltpu

D_MODEL = 1024
BATCH = 8
SEQ = 2048
EPS = 1e-6
CHUNK = 128
E_A = 2 * D_MODEL
G_A = 8
DG_A = E_A // G_A
E_B = 3 * D_MODEL // 2
H_B = 16
BD_B = E_B // H_B
CONV_W = 4
LRU_C = 8.0
H_C = 16
DH_C = D_MODEL // H_C
E_C = H_C * DH_C

LANES = 128
VMEM_LIMIT = 56 * 1024 * 1024

TM_A = 512
WCH_A = 512
T_B = 64
HG_B = 4
GW_B = HG_B * BD_B
TM_C = 512
TQ_C = 256
TK_C = 256
NH_C = 4
DEN_ROWS = 16
NEG_BIG = -1e30
LOG2E = 1.4426950408889634

F32 = jnp.float32
BF16 = jnp.bfloat16


def _dot(a, b):
    return jnp.dot(a, b, preferred_element_type=F32)


def _rms(x, g):
    ms = jnp.mean(x * x, axis=-1, keepdims=True)
    return x * lax.rsqrt(ms + EPS) * g


GELU_C = 0.7978845608028654


def _gelu_of_half(hx):
    return hx * (1.0 + jnp.tanh(hx * (2.0 * GELU_C + (8.0 * GELU_C * 0.044715) * (hx * hx))))


def _silu_of_half(hx):
    return hx * (1.0 + jnp.tanh(hx))


def _softplus(x):
    return jnp.maximum(x, 0.0) + jnp.log1p(jnp.exp(-jnp.abs(x)))


def _const_spec(shape):
    n = len(shape)
    return pl.BlockSpec(shape, lambda *_: (0,) * n, pipeline_mode=pl.Buffered(1))


def _win_chunk_copy(win_hbm, stage_scr, sem, c):
    cols = pl.ds(c * WCH_A, WCH_A)
    return pltpu.make_async_copy(win_hbm.at[:, cols], stage_scr.at[c % 2], sem.at[c % 2])


def _gmlp_kernel(has_pending, x_ref, *refs):
    if has_pending:
        yp_ref, wp_ref, *refs = refs
    nrm_ref, win_hbm, vnrm_ref, ws_ref, bst_ref, wo_ref, o_ref, vn_scr, y_scr, win_ref, stage_scr, sem = refs

    @pl.when(pl.program_id(0) == 0)
    def _():
        n_chunks = 3 * E_A // WCH_A
        _win_chunk_copy(win_hbm, stage_scr, sem, 0).start()
        for c in range(n_chunks):
            if c + 1 < n_chunks:
                _win_chunk_copy(win_hbm, stage_scr, sem, c + 1).start()
            _win_chunk_copy(win_hbm, stage_scr, sem, c).wait()
            win_ref[:, c * WCH_A:(c + 1) * WCH_A] = (0.5 * stage_scr[c % 2]).astype(BF16)

    if has_pending:
        x = x_ref[...] + _dot(yp_ref[...], wp_ref[...])
    else:
        x = x_ref[...]
    h = _rms(x, nrm_ref[...]).astype(BF16)
    v = _gelu_of_half(_dot(h, win_ref[:, E_A:2 * E_A]))
    vn_scr[...] = _rms(v, vnrm_ref[...]).astype(BF16)
    row = lax.broadcasted_iota(jnp.int32, (CHUNK, CHUNK), 0)
    col = lax.broadcasted_iota(jnp.int32, (CHUNK, CHUNK), 1)
    causal = col <= row
    for g in range(G_A):
        c0 = g * DG_A
        u = _gelu_of_half(_dot(h, win_ref[:, c0:c0 + DG_A]))
        gate = _silu_of_half(_dot(h, win_ref[:, 2 * E_A + c0:2 * E_A + c0 + DG_A]))
        w = jnp.where(causal, ws_ref[g], 0.0).astype(BF16)
        bias = bst_ref[:, g:g + 1]
        for c in range(TM_A // CHUNK):
            r0 = c * CHUNK
            mixed = _dot(w, vn_scr[r0:r0 + CHUNK, c0:c0 + DG_A]) + bias
            y = u[r0:r0 + CHUNK] * mixed * gate[r0:r0 + CHUNK]
            y_scr[r0:r0 + CHUNK, c0:c0 + DG_A] = y.astype(BF16)
    o_ref[...] = x + _dot(y_scr[...], wo_ref[...])


def _gmlp_layer(x2, norm, w_in, v_norm, w_s, b_s, w_out, pending=None):
    n = x2.shape[0]
    rows = pl.BlockSpec((TM_A, D_MODEL), lambda i: (i, 0))
    pending_specs, pending_args = [], []
    if pending is not None:
        y_prev, w_prev = pending
        pending_specs = [pl.BlockSpec((TM_A, y_prev.shape[1]), lambda i: (i, 0)), _const_spec(w_prev.shape)]
        pending_args = [y_prev, w_prev.astype(BF16)]
    return pl.pallas_call(
        functools.partial(_gmlp_kernel, pending is not None),
        grid=(n // TM_A,),
        in_specs=[
            rows,
            *pending_specs,
            _const_spec((1, D_MODEL)),
            pl.BlockSpec(memory_space=pl.ANY),
            _const_spec((1, E_A)),
            _const_spec((G_A, CHUNK, CHUNK)),
            _const_spec((CHUNK, G_A)),
            _const_spec((E_A, D_MODEL)),
        ],
        out_specs=rows,
        out_shape=jax.ShapeDtypeStruct((n, D_MODEL), F32),
        scratch_shapes=[pltpu.VMEM((TM_A, E_A), BF16), pltpu.VMEM((TM_A, E_A), BF16),
                        pltpu.VMEM((D_MODEL, 3 * E_A), BF16), pltpu.VMEM((2, D_MODEL, WCH_A), F32),
                        pltpu.SemaphoreType.DMA((2,))],
        compiler_params=pltpu.CompilerParams(dimension_semantics=("arbitrary",), vmem_limit_bytes=VMEM_LIMIT),
        name="gmlp_layer",
    )(x2, *pending_args, norm.reshape(1, D_MODEL), w_in, v_norm.reshape(1, E_A), w_s, b_s.T, w_out.astype(BF16))


ROWS_B = T_B * BATCH
TAIL_B = (CONV_W - 1) * BATCH


def _rglru_kernel(x_ref, nrm_ref, win_ref, cw_ref, cb_ref, wbd_ref, hba_ref, hbx_ref, lam_ref, wo_ref, o_ref,
                  hn_scr, xb_scr, a_scr, hs_scr, state_scr, out_scr):
    @pl.when(pl.program_id(0) == 0)
    def _():
        xb_scr[0:TAIL_B, :] = jnp.zeros((TAIL_B, E_B), F32)
        state_scr[...] = jnp.zeros((BATCH, E_B), F32)

    nrm = nrm_ref[...]
    for b in range(BATCH):
        hb = _rms(x_ref[b], nrm)
        for c in range(D_MODEL // LANES):
            hn_scr[c, pl.ds(b, T_B, stride=BATCH), :] = hb[:, c * LANES:(c + 1) * LANES]
    h = jnp.concatenate([hn_scr[c] for c in range(D_MODEL // LANES)], axis=1).astype(BF16)
    xb_scr[TAIL_B:TAIL_B + ROWS_B, :] = _dot(h, win_ref[:, 0:E_B])
    gate = _silu_of_half(_dot(h, win_ref[:, E_B:2 * E_B]))

    xc = cb_ref[...] + cw_ref[0:1, :] * xb_scr[0:ROWS_B, :]
    for k in range(1, CONV_W):
        xc = xc + cw_ref[k:k + 1, :] * xb_scr[k * BATCH:k * BATCH + ROWS_B, :]
    tail = xb_scr[ROWS_B:ROWS_B + TAIL_B, :]
    xb_scr[0:TAIL_B, :] = tail
    xcb = xc.astype(BF16)

    quarter_c_sp = (0.25 * LRU_C) * _softplus(-lam_ref[...])
    for j in range(E_B // GW_B):
        c0 = j * GW_B
        pre = _dot(xcb[:, c0:c0 + GW_B], wbd_ref[j])
        ur = jnp.tanh(pre[:, 0:GW_B] + hba_ref[:, c0:c0 + GW_B])
        ui = jnp.tanh(pre[:, GW_B:2 * GW_B] + hbx_ref[:, c0:c0 + GW_B])
        tn = jnp.tanh(quarter_c_sp[:, c0:c0 + GW_B] * (1.0 + ur))
        d = 1.0 / (1.0 + tn)
        a_scr[:, c0:c0 + GW_B] = (1.0 - tn) * d
        root = jnp.where(tn <= 0.0, 0.0, tn * lax.rsqrt(tn))
        hs_scr[:, c0:c0 + GW_B] = (d * root) * ((1.0 + ui) * xc[:, c0:c0 + GW_B])

    hcur = state_scr[...]
    for t in range(T_B):
        rows = slice(t * BATCH, (t + 1) * BATCH)
        hcur = a_scr[rows, :] * hcur + hs_scr[rows, :]
        hs_scr[rows, :] = hcur
    state_scr[...] = hcur

    y = (hs_scr[...] * gate).astype(BF16)
    out = _dot(y, wo_ref[...])
    for c in range(D_MODEL // LANES):
        out_scr[c] = out[:, c * LANES:(c + 1) * LANES]
    for b in range(BATCH):
        ob = jnp.concatenate([out_scr[c, pl.ds(b, T_B, stride=BATCH), :] for c in range(D_MODEL // LANES)], axis=1)
        o_ref[b] = x_ref[b] + ob


def _block_diag_gates(w_a, w_x):
    def bd(w):
        wg = w.reshape(H_B // HG_B, HG_B, BD_B, BD_B)
        eye = jnp.eye(HG_B, dtype=w.dtype)
        return jnp.einsum('ghij,hk->ghikj', wg, eye).reshape(H_B // HG_B, GW_B, GW_B)
    return jnp.concatenate([bd(w_a), bd(w_x)], axis=-1)


def _rglru_layer(x3, norm, w_in, conv_w, conv_b, w_a, b_a, w_x, b_x, lam, w_out):
    ng = H_B // HG_B
    w_half = jnp.concatenate([w_in[:, 0:E_B], 0.5 * w_in[:, E_B:2 * E_B]], axis=1)
    return pl.pallas_call(
        _rglru_kernel,
        grid=(SEQ // T_B,),
        in_specs=[
            pl.BlockSpec((BATCH, T_B, D_MODEL), lambda i: (0, i, 0)),
            _const_spec((1, D_MODEL)),
            _const_spec((D_MODEL, 2 * E_B)),
            _const_spec((CONV_W, E_B)),
            _const_spec((1, E_B)),
            _const_spec((ng, GW_B, 2 * GW_B)),
            _const_spec((1, E_B)),
            _const_spec((1, E_B)),
            _const_spec((1, E_B)),
            _const_spec((E_B, D_MODEL)),
        ],
        out_specs=pl.BlockSpec((BATCH, T_B, D_MODEL), lambda i: (0, i, 0)),
        out_shape=jax.ShapeDtypeStruct((BATCH, SEQ, D_MODEL), F32),
        scratch_shapes=[
            pltpu.VMEM((D_MODEL // LANES, ROWS_B, LANES), F32),
            pltpu.VMEM((TAIL_B + ROWS_B, E_B), F32),
            pltpu.VMEM((ROWS_B, E_B), F32),
            pltpu.VMEM((ROWS_B, E_B), F32),
            pltpu.VMEM((BATCH, E_B), F32),
            pltpu.VMEM((D_MODEL // LANES, ROWS_B, LANES), F32),
        ],
        compiler_params=pltpu.CompilerParams(dimension_semantics=("arbitrary",), vmem_limit_bytes=VMEM_LIMIT),
        name="rglru_layer",
    )(x3, norm.reshape(1, D_MODEL), w_half.astype(BF16), conv_w, conv_b.reshape(1, E_B),
      (0.5 * _block_diag_gates(w_a, w_x)).astype(BF16), (0.5 * b_a).reshape(1, E_B), (0.5 * b_x).reshape(1, E_B),
      lam.reshape(1, E_B), w_out.astype(BF16))


N_SPLIT = 3


def _fox_proj_kernel(x_ref, nrm_ref, wkg_ref, wqvt_ref, wf_ref, bf_ref, kg_ref, seg_ref, segt_ref, place_ref, tri_ref,
                     qt_ref, k_ref, ek_ref, vt_ref, sg_ref, cum_ref, carry_scr):
    @pl.when(pl.program_id(1) == 0)
    def _():
        carry_scr[...] = jnp.zeros((1, LANES), F32)

    h = _rms(x_ref[0], nrm_ref[...]).astype(BF16)

    zk = _dot(h, wkg_ref[:, 0:E_C])
    ss = _dot((zk * zk).astype(BF16), seg_ref[...])
    r = lax.rsqrt(ss * (1.0 / DH_C) + EPS)
    r_hi = r.astype(BF16)
    r_lo = (r - r_hi.astype(F32)).astype(BF16)
    rexp = _dot(jnp.concatenate([r_hi, r_lo], axis=1), segt_ref[...])
    k_ref[0] = (zk * rexp * kg_ref[...]).astype(BF16)
    sg_ref[0] = _silu_of_half(_dot(h, wkg_ref[:, E_C:2 * E_C])).astype(BF16)

    zqv = lax.dot_general(wqvt_ref[...], h, (((1,), (1,)), ((), ())), preferred_element_type=F32)
    vt_ref[0] = zqv[E_C:2 * E_C].astype(BF16)
    for hd in range(H_C):
        rows = slice(hd * DH_C, (hd + 1) * DH_C)
        zq = zqv[rows]
        rq = lax.rsqrt(jnp.mean(zq * zq, axis=0, keepdims=True) + EPS)
        qt_ref[0, rows, :] = (zq * rq).astype(BF16)

    z = _dot(h, wf_ref[...]) + bf_ref[...]
    log_f = jnp.minimum(z, 0.0) - jnp.log1p(jnp.exp(-jnp.abs(z)))
    group = lax.broadcasted_iota(jnp.int32, (TM_C, LANES), 1) // H_C

    def split3(v):
        hi = v.astype(BF16).astype(F32)
        mid = (v - hi).astype(BF16).astype(F32)
        lo = v - hi - mid
        return jnp.where(group == 0, hi, jnp.where(group == 1, mid, lo)).astype(BF16)

    cp = _dot(tri_ref[...], split3(log_f))
    cum = cp + pltpu.roll(cp, LANES - H_C, axis=1) + pltpu.roll(cp, LANES - 2 * H_C, axis=1) + carry_scr[...]
    carry_scr[...] = cum[TM_C - 1:TM_C, :]
    c2 = cum * LOG2E
    cum_ref[0] = c2[:, 0:H_C]
    c2 = jnp.where(group == 0, c2,
                   jnp.where(group == 1, pltpu.roll(c2, H_C, axis=1), pltpu.roll(c2, 2 * H_C, axis=1)))
    ek_ref[0] = _dot(split3(c2), place_ref[...]).astype(BF16)


def _fox_proj(x3, norm, w_in, b_f, q_norm, k_norm):
    w_kg = jnp.concatenate([w_in[:, E_C:2 * E_C], 0.5 * w_in[:, 3 * E_C:4 * E_C]], axis=1).astype(BF16)
    w_qvt = jnp.concatenate([w_in[:, 0:E_C], w_in[:, 2 * E_C:3 * E_C]], axis=1).T.astype(BF16)
    pad = LANES - N_SPLIT * H_C
    w_f = jnp.pad(jnp.tile(w_in[:, 4 * E_C:], (1, N_SPLIT)), ((0, 0), (0, pad))).astype(BF16)
    b_fp = jnp.pad(jnp.tile(b_f, N_SPLIT), (0, pad)).reshape(1, LANES)
    kg = jnp.tile(k_norm * q_norm, H_C).reshape(1, E_C) * (DH_C ** -0.5 * LOG2E)
    head_of = jnp.arange(E_C) // DH_C
    seg = (head_of[:, None] == jnp.arange(LANES)[None, :]).astype(BF16)
    segt2 = jnp.concatenate([seg.T, seg.T], axis=0)
    src = jnp.arange(LANES)
    dst = (src % H_C) * DH_C + src // H_C
    place = jnp.where((src[:, None] < N_SPLIT * H_C) & (dst[:, None] == jnp.arange(E_C)[None, :]), -1.0, 0.0)
    tok = jax.ShapeDtypeStruct((BATCH, SEQ, E_C), BF16)
    tok_t = jax.ShapeDtypeStruct((BATCH, E_C, SEQ), BF16)
    blk = pl.BlockSpec((1, TM_C, E_C), lambda b, i: (b, i, 0))
    blk_t = pl.BlockSpec((1, E_C, TM_C), lambda b, i: (b, 0, i))
    return pl.pallas_call(
        _fox_proj_kernel,
        grid=(BATCH, SEQ // TM_C),
        in_specs=[
            pl.BlockSpec((1, TM_C, D_MODEL), lambda b, i: (b, i, 0)),
            _const_spec((1, D_MODEL)),
            _const_spec((D_MODEL, 2 * E_C)),
            _const_spec((2 * E_C, D_MODEL)),
            _const_spec((D_MODEL, LANES)),
            _const_spec((1, LANES)),
            _const_spec((1, E_C)),
            _const_spec((E_C, LANES)),
            _const_spec((2 * LANES, E_C)),
            _const_spec((LANES, E_C)),
            _const_spec((TM_C, TM_C)),
        ],
        out_specs=[blk_t, blk, blk, blk_t, blk, pl.BlockSpec((1, TM_C, H_C), lambda b, i: (b, i, 0))],
        out_shape=[tok_t, tok, tok, tok_t, tok, jax.ShapeDtypeStruct((BATCH, SEQ, H_C), F32)],
        scratch_shapes=[pltpu.VMEM((1, LANES), F32)],
        compiler_params=pltpu.CompilerParams(dimension_semantics=("arbitrary", "arbitrary"),
                                             vmem_limit_bytes=VMEM_LIMIT),
        name="fox_proj",
    )(x3, norm.reshape(1, D_MODEL), w_kg, w_qvt, w_f, b_fp, kg, seg, segt2, place.astype(BF16),
      jnp.tril(jnp.ones((TM_C, TM_C), BF16)))


def _attn_kernel(qt_ref, k_ref, ek_ref, vt_ref, sg_ref, cq_ref, o_ref,
                 qa_scr, acc_scr, m_scr, sta_scr, stb_scr, mxa_scr, mxb_scr):
    causal = (lax.broadcasted_iota(jnp.int32, (TK_C, TQ_C), 0) <= lax.broadcasted_iota(jnp.int32, (TK_C, TQ_C), 1))
    ones_v = jnp.ones((DEN_ROWS, TK_C), BF16)
    zero_q = jnp.zeros((DH_C, TQ_C), BF16)
    row_q = lax.broadcasted_iota(jnp.int32, (LANES, TQ_C), 0)
    heads = range(NH_C)

    def load_queries(qi):
        cols = slice(qi * TQ_C, (qi + 1) * TQ_C)
        for hh in heads:
            odd = hh % 2
            qt = qt_ref[0, hh * DH_C:(hh + 1) * DH_C, cols]
            qa_scr[hh, 0:LANES, :] = jnp.concatenate([zero_q, qt] if odd else [qt, zero_q], axis=0)
            mine = jnp.logical_and(row_q >= odd * DH_C, row_q < odd * DH_C + N_SPLIT)
            qa_scr[hh, LANES:2 * LANES, :] = jnp.where(mine, 1.0, 0.0).astype(BF16)

    def scores(qi, j, st_ref, mx_ref):
        keys = slice(j * TK_C, (j + 1) * TK_C)
        for hh in heads:
            pair = hh // 2
            ka = jnp.concatenate([k_ref[0, keys, pair * LANES:(pair + 1) * LANES],
                                  ek_ref[0, keys, pair * LANES:(pair + 1) * LANES]], axis=1)
            st = _dot(ka, qa_scr[hh])
            if j == qi:
                st = jnp.where(causal, st, NEG_BIG)
            st_ref[hh] = st
            mx_ref[hh] = jnp.max(st, axis=0, keepdims=True)

    def consume(qi, j, st_ref, mx_ref):
        alpha, pt = [], []
        for hh in heads:
            cq = cq_ref[0, 0, hh:hh + 1, qi * TQ_C:(qi + 1) * TQ_C]
            m_old = m_scr[hh]
            m = jnp.maximum(m_old, mx_ref[hh] + cq)
            pt.append(jnp.exp2((st_ref[hh] - (m - cq)).astype(BF16)))
            alpha.append(jnp.exp2(m_old - m))
            m_scr[hh] = m
        pv = []
        for hh in heads:
            vt = vt_ref[0, hh * DH_C:(hh + 1) * DH_C, j * TK_C:(j + 1) * TK_C]
            pv.append(_dot(jnp.concatenate([vt, ones_v], axis=0), pt[hh]))
        for hh in heads:
            acc_scr[hh] = alpha[hh] * acc_scr[hh] + pv[hh]

    def finish(qi):
        rows = slice(qi * TQ_C, (qi + 1) * TQ_C)
        for pair in range(NH_C // 2):
            a0 = acc_scr[2 * pair]
            a1 = acc_scr[2 * pair + 1]
            ot = jnp.concatenate([a0[0:DH_C] / a0[DH_C:DH_C + 1], a1[0:DH_C] / a1[DH_C:DH_C + 1]], axis=0)
            cols = slice(pair * LANES, (pair + 1) * LANES)
            o_ref[0, rows, cols] = (ot.T * sg_ref[0, rows, cols].astype(F32)).astype(BF16)

    tiles = [(qi, j) for qi in range(SEQ // TQ_C) for j in range(qi + 1)]
    bufs = ((sta_scr, mxa_scr), (stb_scr, mxb_scr))
    def issue(n):
        qi, j = tiles[n]
        if j == 0:
            load_queries(qi)
        scores(qi, j, *bufs[n % 2])

    issue(0)
    issue(1)
    for n, (qi, j) in enumerate(tiles):
        if j == 0:
            acc_scr[...] = jnp.zeros(acc_scr.shape, F32)
            m_scr[...] = jnp.full(m_scr.shape, NEG_BIG, F32)
        consume(qi, j, *bufs[n % 2])
        if j == qi:
            finish(qi)
        if n + 2 < len(tiles):
            issue(n + 2)


def _fox_attention(qt, k, ek, vt, sg, cum):
    assert TQ_C == TK_C
    ng = H_C // NH_C
    w = NH_C * DH_C
    cq = cum.transpose(0, 2, 1).reshape(BATCH, ng, NH_C, SEQ)
    blk = pl.BlockSpec((1, SEQ, w), lambda b, g: (b, 0, g))
    blk_t = pl.BlockSpec((1, w, SEQ), lambda b, g: (b, g, 0))
    return pl.pallas_call(
        _attn_kernel,
        grid=(BATCH, ng),
        in_specs=[
            blk_t, blk, blk, blk_t, blk,
            pl.BlockSpec((1, 1, NH_C, SEQ), lambda b, g: (b, g, 0, 0)),
        ],
        out_specs=blk,
        out_shape=jax.ShapeDtypeStruct((BATCH, SEQ, E_C), BF16),
        scratch_shapes=[pltpu.VMEM((NH_C, 2 * LANES, TQ_C), BF16), pltpu.VMEM((NH_C, DH_C + DEN_ROWS, TQ_C), F32),
                        pltpu.VMEM((NH_C, 1, TQ_C), F32),
                        pltpu.VMEM((NH_C, TK_C, TQ_C), F32), pltpu.VMEM((NH_C, TK_C, TQ_C), F32),
                        pltpu.VMEM((NH_C, 1, TQ_C), F32), pltpu.VMEM((NH_C, 1, TQ_C), F32)],
        compiler_params=pltpu.CompilerParams(dimension_semantics=("arbitrary", "arbitrary"),
                                             vmem_limit_bytes=VMEM_LIMIT),
        name="fox_attention",
    )(qt, k, ek, vt, sg, cq)


def _fox_mixer(x3, norm, w_in, b_f, q_norm, k_norm):
    qt, k, ek, vt, sg, cum = _fox_proj(x3, norm, w_in, b_f, q_norm, k_norm)
    return _fox_attention(qt, k, ek, vt, sg, cum)


def kernel(x, l0_norm, l0_w_in, l0_v_norm, l0_w_s, l0_b_s, l0_w_out, l1_norm, l1_w_in, l1_conv_w, l1_conv_b, l1_w_a, l1_b_a, l1_w_x, l1_b_x, l1_lam, l1_w_out, l2_norm, l2_w_in, l2_b_f, l2_q_norm, l2_k_norm, l2_w_out, l3_norm, l3_w_in, l3_v_norm, l3_w_s, l3_b_s, l3_w_out):
    n = BATCH * SEQ
    x = _gmlp_layer(x.reshape(n, D_MODEL), l0_norm, l0_w_in, l0_v_norm, l0_w_s, l0_b_s, l0_w_out)
    x = _rglru_layer(x.reshape(BATCH, SEQ, D_MODEL), l1_norm, l1_w_in, l1_conv_w, l1_conv_b, l1_w_a, l1_b_a,
                     l1_w_x, l1_b_x, l1_lam, l1_w_out)
    y = _fox_mixer(x, l2_norm, l2_w_in, l2_b_f, l2_q_norm, l2_k_norm)
    x = _gmlp_layer(x.reshape(n, D_MODEL), l3_norm, l3_w_in, l3_v_norm, l3_w_s, l3_b_s, l3_w_out,
                    pending=(y.reshape(n, E_C), l2_w_out))
    return x.reshape(BATCH, SEQ, D_MODEL)
```

```python
import functools

import jax
import jax.numpy as jnp
from jax import lax
from jax.experimental import pallas as pl
from jax.experimental.pallas import tpu as pltpu

D_MODEL = 1024
BATCH = 8
SEQ = 2048
EPS = 1e-6
CHUNK = 128
E_A = 2 * D_MODEL
G_A = 8
DG_A = E_A // G_A
E_B = 3 * D_MODEL // 2
H_B = 16
BD_B = E_B // H_B
CONV_W = 4
LRU_C = 8.0
H_C = 16
DH_C = D_MODEL // H_C
E_C = H_C * DH_C

LANES = 128
VMEM_LIMIT = 56 * 1024 * 1024

TM_A = 512
WCH_A = 512
T_B = 64
HG_B = 4
GW_B = HG_B * BD_B
TM_C = 1024
TQ_C = 256
TK_C = 256
NH_C = 4
DEN_ROWS = 16
NEG_BIG = -1e30
LOG2E = 1.4426950408889634

F32 = jnp.float32
BF16 = jnp.bfloat16


def _dot(a, b):
    return jnp.dot(a, b, preferred_element_type=F32)


def _rms(x, g):
    ms = jnp.mean(x * x, axis=-1, keepdims=True)
    return x * lax.rsqrt(ms + EPS) * g


GELU_C = 0.7978845608028654


def _gelu_of_half(hx):
    return hx * (1.0 + jnp.tanh(hx * (2.0 * GELU_C + (8.0 * GELU_C * 0.044715) * (hx * hx))))


def _silu_of_half(hx):
    return hx * (1.0 + jnp.tanh(hx))


def _softplus(x):
    return jnp.maximum(x, 0.0) + jnp.log1p(jnp.exp(-jnp.abs(x)))


def _const_spec(shape):
    n = len(shape)
    return pl.BlockSpec(shape, lambda *_: (0,) * n, pipeline_mode=pl.Buffered(1))


def _win_chunk_copy(win_hbm, stage_scr, sem, c):
    cols = pl.ds(c * WCH_A, WCH_A)
    return pltpu.make_async_copy(win_hbm.at[:, cols], stage_scr.at[c % 2], sem.at[c % 2])


def _gmlp_kernel(has_pending, x_ref, *refs):
    if has_pending:
        yp_ref, wp_ref, *refs = refs
    nrm_ref, win_hbm, vnrm_ref, ws_ref, bst_ref, wo_ref, o_ref, vn_scr, y_scr, win_ref, stage_scr, sem = refs

    @pl.when(pl.program_id(0) == 0)
    def _():
        n_chunks = 3 * E_A // WCH_A
        _win_chunk_copy(win_hbm, stage_scr, sem, 0).start()
        for c in range(n_chunks):
            if c + 1 < n_chunks:
                _win_chunk_copy(win_hbm, stage_scr, sem, c + 1).start()
            _win_chunk_copy(win_hbm, stage_scr, sem, c).wait()
            win_ref[:, c * WCH_A:(c + 1) * WCH_A] = (0.5 * stage_scr[c % 2]).astype(BF16)

    if has_pending:
        x = x_ref[...] + _dot(yp_ref[...], wp_ref[...])
    else:
        x = x_ref[...]
    h = _rms(x, nrm_ref[...]).astype(BF16)
    v = _gelu_of_half(_dot(h, win_ref[:, E_A:2 * E_A]))
    vn_scr[...] = _rms(v, vnrm_ref[...]).astype(BF16)
    row = lax.broadcasted_iota(jnp.int32, (CHUNK, CHUNK), 0)
    col = lax.broadcasted_iota(jnp.int32, (CHUNK, CHUNK), 1)
    causal = col <= row
    for g in range(G_A):
        c0 = g * DG_A
        u = _gelu_of_half(_dot(h, win_ref[:, c0:c0 + DG_A]))
        gate = _silu_of_half(_dot(h, win_ref[:, 2 * E_A + c0:2 * E_A + c0 + DG_A]))
        w = jnp.where(causal, ws_ref[g], 0.0).astype(BF16)
        bias = bst_ref[:, g:g + 1]
        for c in range(TM_A // CHUNK):
            r0 = c * CHUNK
            mixed = _dot(w, vn_scr[r0:r0 + CHUNK, c0:c0 + DG_A]) + bias
            y = u[r0:r0 + CHUNK] * mixed * gate[r0:r0 + CHUNK]
            y_scr[r0:r0 + CHUNK, c0:c0 + DG_A] = y.astype(BF16)
    o_ref[...] = x + _dot(y_scr[...], wo_ref[...])


def _gmlp_layer(x2, norm, w_in, v_norm, w_s, b_s, w_out, pending=None):
    n = x2.shape[0]
    rows = pl.BlockSpec((TM_A, D_MODEL), lambda i: (i, 0))
    pending_specs, pending_args = [], []
    if pending is not None:
        y_prev, w_prev = pending
        pending_specs = [pl.BlockSpec((TM_A, y_prev.shape[1]), lambda i: (i, 0)), _const_spec(w_prev.shape)]
        pending_args = [y_prev, w_prev.astype(BF16)]
    return pl.pallas_call(
        functools.partial(_gmlp_kernel, pending is not None),
        grid=(n // TM_A,),
        in_specs=[
            rows,
            *pending_specs,
            _const_spec((1, D_MODEL)),
            pl.BlockSpec(memory_space=pl.ANY),
            _const_spec((1, E_A)),
            _const_spec((G_A, CHUNK, CHUNK)),
            _const_spec((CHUNK, G_A)),
            _const_spec((E_A, D_MODEL)),
        ],
        out_specs=rows,
        out_shape=jax.ShapeDtypeStruct((n, D_MODEL), F32),
        scratch_shapes=[pltpu.VMEM((TM_A, E_A), BF16), pltpu.VMEM((TM_A, E_A), BF16),
                        pltpu.VMEM((D_MODEL, 3 * E_A), BF16), pltpu.VMEM((2, D_MODEL, WCH_A), F32),
                        pltpu.SemaphoreType.DMA((2,))],
        compiler_params=pltpu.CompilerParams(dimension_semantics=("arbitrary",), vmem_limit_bytes=VMEM_LIMIT),
        name="gmlp_layer",
    )(x2, *pending_args, norm.reshape(1, D_MODEL), w_in, v_norm.reshape(1, E_A), w_s, b_s.T, w_out.astype(BF16))


ROWS_B = T_B * BATCH
TAIL_B = (CONV_W - 1) * BATCH


def _rglru_kernel(x_ref, nrm_ref, win_ref, cw_ref, cb_ref, wbd_ref, hba_ref, hbx_ref, lam_ref, wo_ref, o_ref,
                  hn_scr, xb_scr, a_scr, hs_scr, state_scr, out_scr):
    @pl.when(pl.program_id(0) == 0)
    def _():
        xb_scr[0:TAIL_B, :] = jnp.zeros((TAIL_B, E_B), F32)
        state_scr[...] = jnp.zeros((BATCH, E_B), F32)

    nrm = nrm_ref[...]
    for b in range(BATCH):
        hb = _rms(x_ref[b], nrm)
        for c in range(D_MODEL // LANES):
            hn_scr[c, pl.ds(b, T_B, stride=BATCH), :] = hb[:, c * LANES:(c + 1) * LANES]
    h = jnp.concatenate([hn_scr[c] for c in range(D_MODEL // LANES)], axis=1).astype(BF16)
    xb_scr[TAIL_B:TAIL_B + ROWS_B, :] = _dot(h, win_ref[:, 0:E_B])
    gate = _silu_of_half(_dot(h, win_ref[:, E_B:2 * E_B]))

    xc = cb_ref[...] + cw_ref[0:1, :] * xb_scr[0:ROWS_B, :]
    for k in range(1, CONV_W):
        xc = xc + cw_ref[k:k + 1, :] * xb_scr[k * BATCH:k * BATCH + ROWS_B, :]
    tail = xb_scr[ROWS_B:ROWS_B + TAIL_B, :]
    xb_scr[0:TAIL_B, :] = tail
    xcb = xc.astype(BF16)

    quarter_c_sp = (0.25 * LRU_C) * _softplus(-lam_ref[...])
    for j in range(E_B // GW_B):
        c0 = j * GW_B
        pre = _dot(xcb[:, c0:c0 + GW_B], wbd_ref[j])
        ur = jnp.tanh(pre[:, 0:GW_B] + hba_ref[:, c0:c0 + GW_B])
        ui = jnp.tanh(pre[:, GW_B:2 * GW_B] + hbx_ref[:, c0:c0 + GW_B])
        tn = jnp.tanh(quarter_c_sp[:, c0:c0 + GW_B] * (1.0 + ur))
        d = 1.0 / (1.0 + tn)
        a_scr[:, c0:c0 + GW_B] = (1.0 - tn) * d
        root = jnp.where(tn <= 0.0, 0.0, tn * lax.rsqrt(tn))
        hs_scr[:, c0:c0 + GW_B] = (d * root) * ((1.0 + ui) * xc[:, c0:c0 + GW_B])

    hcur = state_scr[...]
    for t in range(T_B):
        rows = slice(t * BATCH, (t + 1) * BATCH)
        hcur = a_scr[rows, :] * hcur + hs_scr[rows, :]
        hs_scr[rows, :] = hcur
    state_scr[...] = hcur

    y = (hs_scr[...] * gate).astype(BF16)
    out = _dot(y, wo_ref[...])
    for c in range(D_MODEL // LANES):
        out_scr[c] = out[:, c * LANES:(c + 1) * LANES]
    for b in range(BATCH):
        ob = jnp.concatenate([out_scr[c, pl.ds(b, T_B, stride=BATCH), :] for c in range(D_MODEL // LANES)], axis=1)
        o_ref[b] = x_ref[b] + ob


def _block_diag_gates(w_a, w_x):
    def bd(w):
        wg = w.reshape(H_B // HG_B, HG_B, BD_B, BD_B)
        eye = jnp.eye(HG_B, dtype=w.dtype)
        return jnp.einsum('ghij,hk->ghikj', wg, eye).reshape(H_B // HG_B, GW_B, GW_B)
    return jnp.concatenate([bd(w_a), bd(w_x)], axis=-1)


def _rglru_layer(x3, norm, w_in, conv_w, conv_b, w_a, b_a, w_x, b_x, lam, w_out):
    ng = H_B // HG_B
    w_half = jnp.concatenate([w_in[:, 0:E_B], 0.5 * w_in[:, E_B:2 * E_B]], axis=1)
    return pl.pallas_call(
        _rglru_kernel,
        grid=(SEQ // T_B,),
        in_specs=[
            pl.BlockSpec((BATCH, T_B, D_MODEL), lambda i: (0, i, 0)),
            _const_spec((1, D_MODEL)),
            _const_spec((D_MODEL, 2 * E_B)),
            _const_spec((CONV_W, E_B)),
            _const_spec((1, E_B)),
            _const_spec((ng, GW_B, 2 * GW_B)),
            _const_spec((1, E_B)),
            _const_spec((1, E_B)),
            _const_spec((1, E_B)),
            _const_spec((E_B, D_MODEL)),
        ],
        out_specs=pl.BlockSpec((BATCH, T_B, D_MODEL), lambda i: (0, i, 0)),
        out_shape=jax.ShapeDtypeStruct((BATCH, SEQ, D_MODEL), F32),
        scratch_shapes=[
            pltpu.VMEM((D_MODEL // LANES, ROWS_B, LANES), F32),
            pltpu.VMEM((TAIL_B + ROWS_B, E_B), F32),
            pltpu.VMEM((ROWS_B, E_B), F32),
            pltpu.VMEM((ROWS_B, E_B), F32),
            pltpu.VMEM((BATCH, E_B), F32),
            pltpu.VMEM((D_MODEL // LANES, ROWS_B, LANES), F32),
        ],
        compiler_params=pltpu.CompilerParams(dimension_semantics=("arbitrary",), vmem_limit_bytes=VMEM_LIMIT),
        name="rglru_layer",
    )(x3, norm.reshape(1, D_MODEL), w_half.astype(BF16), conv_w, conv_b.reshape(1, E_B),
      (0.5 * _block_diag_gates(w_a, w_x)).astype(BF16), (0.5 * b_a).reshape(1, E_B), (0.5 * b_x).reshape(1, E_B),
      lam.reshape(1, E_B), w_out.astype(BF16))


N_SPLIT = 3


def _fox_proj_kernel(x_ref, nrm_ref, wkg_ref, wqvt_ref, wf_ref, bf_ref, kg_ref, seg_ref, segt_ref, place_ref, tri_ref,
                     qt_ref, k_ref, ek_ref, vt_ref, sg_ref, cum_ref, carry_scr):
    @pl.when(pl.program_id(1) == 0)
    def _():
        carry_scr[...] = jnp.zeros((1, LANES), F32)

    h = _rms(x_ref[0], nrm_ref[...]).astype(BF16)

    zk = _dot(h, wkg_ref[:, 0:E_C])
    ss = _dot((zk * zk).astype(BF16), seg_ref[...])
    r = lax.rsqrt(ss * (1.0 / DH_C) + EPS)
    r_hi = r.astype(BF16)
    r_lo = (r - r_hi.astype(F32)).astype(BF16)
    rexp = _dot(jnp.concatenate([r_hi, r_lo], axis=1), segt_ref[...])
    k_ref[0] = (zk * rexp * kg_ref[...]).astype(BF16)
    sg_ref[0] = _silu_of_half(_dot(h, wkg_ref[:, E_C:2 * E_C])).astype(BF16)

    zqv = lax.dot_general(wqvt_ref[...], h, (((1,), (1,)), ((), ())), preferred_element_type=F32)
    vt_ref[0] = zqv[E_C:2 * E_C].astype(BF16)
    for hd in range(H_C):
        rows = slice(hd * DH_C, (hd + 1) * DH_C)
        zq = zqv[rows]
        rq = lax.rsqrt(jnp.mean(zq * zq, axis=0, keepdims=True) + EPS)
        qt_ref[0, rows, :] = (zq * rq).astype(BF16)

    z = _dot(h, wf_ref[...]) + bf_ref[...]
    log_f = jnp.minimum(z, 0.0) - jnp.log1p(jnp.exp(-jnp.abs(z)))
    group = lax.broadcasted_iota(jnp.int32, (TM_C, LANES), 1) // H_C

    def split3(v):
        hi = v.astype(BF16).astype(F32)
        mid = (v - hi).astype(BF16).astype(F32)
        lo = v - hi - mid
        return jnp.where(group == 0, hi, jnp.where(group == 1, mid, lo)).astype(BF16)

    cp = _dot(tri_ref[...], split3(log_f))
    cum = cp + pltpu.roll(cp, LANES - H_C, axis=1) + pltpu.roll(cp, LANES - 2 * H_C, axis=1) + carry_scr[...]
    carry_scr[...] = cum[TM_C - 1:TM_C, :]
    c2 = cum * LOG2E
    cum_ref[0] = c2[:, 0:H_C]
    c2 = jnp.where(group == 0, c2,
                   jnp.where(group == 1, pltpu.roll(c2, H_C, axis=1), pltpu.roll(c2, 2 * H_C, axis=1)))
    ek_ref[0] = _dot(split3(c2), place_ref[...]).astype(BF16)


def _fox_proj(x3, norm, w_in, b_f, q_norm, k_norm):
    w_kg = jnp.concatenate([w_in[:, E_C:2 * E_C], 0.5 * w_in[:, 3 * E_C:4 * E_C]], axis=1).astype(BF16)
    w_qvt = jnp.concatenate([w_in[:, 0:E_C], w_in[:, 2 * E_C:3 * E_C]], axis=1).T.astype(BF16)
    pad = LANES - N_SPLIT * H_C
    w_f = jnp.pad(jnp.tile(w_in[:, 4 * E_C:], (1, N_SPLIT)), ((0, 0), (0, pad))).astype(BF16)
    b_fp = jnp.pad(jnp.tile(b_f, N_SPLIT), (0, pad)).reshape(1, LANES)
    kg = jnp.tile(k_norm * q_norm, H_C).reshape(1, E_C) * (DH_C ** -0.5 * LOG2E)
    head_of = jnp.arange(E_C) // DH_C
    seg = (head_of[:, None] == jnp.arange(LANES)[None, :]).astype(BF16)
    segt2 = jnp.concatenate([seg.T, seg.T], axis=0)
    src = jnp.arange(LANES)
    dst = (src % H_C) * DH_C + src // H_C
    place = jnp.where((src[:, None] < N_SPLIT * H_C) & (dst[:, None] == jnp.arange(E_C)[None, :]), -1.0, 0.0)
    tok = jax.ShapeDtypeStruct((BATCH, SEQ, E_C), BF16)
    tok_t = jax.ShapeDtypeStruct((BATCH, E_C, SEQ), BF16)
    blk = pl.BlockSpec((1, TM_C, E_C), lambda b, i: (b, i, 0))
    blk_t = pl.BlockSpec((1, E_C, TM_C), lambda b, i: (b, 0, i))
    return pl.pallas_call(
        _fox_proj_kernel,
        grid=(BATCH, SEQ // TM_C),
        in_specs=[
            pl.BlockSpec((1, TM_C, D_MODEL), lambda b, i: (b, i, 0)),
            _const_spec((1, D_MODEL)),
            _const_spec((D_MODEL, 2 * E_C)),
            _const_spec((2 * E_C, D_MODEL)),
            _const_spec((D_MODEL, LANES)),
            _const_spec((1, LANES)),
            _const_spec((1, E_C)),
            _const_spec((E_C, LANES)),
            _const_spec((2 * LANES, E_C)),
            _const_spec((LANES, E_C)),
            _const_spec((TM_C, TM_C)),
        ],
        out_specs=[blk_t, blk, blk, blk_t, blk, pl.BlockSpec((1, TM_C, H_C), lambda b, i: (b, i, 0))],
        out_shape=[tok_t, tok, tok, tok_t, tok, jax.ShapeDtypeStruct((BATCH, SEQ, H_C), F32)],
        scratch_shapes=[pltpu.VMEM((1, LANES), F32)],
        compiler_params=pltpu.CompilerParams(dimension_semantics=("arbitrary", "arbitrary"),
                                             vmem_limit_bytes=VMEM_LIMIT),
        name="fox_proj",
    )(x3, norm.reshape(1, D_MODEL), w_kg, w_qvt, w_f, b_fp, kg, seg, segt2, place.astype(BF16),
      jnp.tril(jnp.ones((TM_C, TM_C), BF16)))


def _attn_kernel(qt_ref, k_ref, ek_ref, vt_ref, sg_ref, cq_ref, o_ref,
                 qa_scr, acc_scr, m_scr, sta_scr, stb_scr, mxa_scr, mxb_scr):
    causal = (lax.broadcasted_iota(jnp.int32, (TK_C, TQ_C), 0) <= lax.broadcasted_iota(jnp.int32, (TK_C, TQ_C), 1))
    ones_v = jnp.ones((DEN_ROWS, TK_C), BF16)
    zero_q = jnp.zeros((DH_C, TQ_C), BF16)
    row_q = lax.broadcasted_iota(jnp.int32, (LANES, TQ_C), 0)
    heads = range(NH_C)

    def load_queries(qi):
        cols = slice(qi * TQ_C, (qi + 1) * TQ_C)
        for hh in heads:
            odd = hh % 2
            qt = qt_ref[0, hh * DH_C:(hh + 1) * DH_C, cols]
            qa_scr[hh, 0:LANES, :] = jnp.concatenate([zero_q, qt] if odd else [qt, zero_q], axis=0)
            mine = jnp.logical_and(row_q >= odd * DH_C, row_q < odd * DH_C + N_SPLIT)
            qa_scr[hh, LANES:2 * LANES, :] = jnp.where(mine, 1.0, 0.0).astype(BF16)

    def scores(qi, j, st_ref, mx_ref):
        keys = slice(j * TK_C, (j + 1) * TK_C)
        for hh in heads:
            pair = hh // 2
            ka = jnp.concatenate([k_ref[0, keys, pair * LANES:(pair + 1) * LANES],
                                  ek_ref[0, keys, pair * LANES:(pair + 1) * LANES]], axis=1)
            st = _dot(ka, qa_scr[hh])
            if j == qi:
                st = jnp.where(causal, st, NEG_BIG)
            st_ref[hh] = st
            mx_ref[hh] = jnp.max(st, axis=0, keepdims=True)

    def consume(qi, j, st_ref, mx_ref):
        alpha, pt = [], []
        for hh in heads:
            cq = cq_ref[0, 0, hh:hh + 1, qi * TQ_C:(qi + 1) * TQ_C]
            m_old = m_scr[hh]
            m = jnp.maximum(m_old, mx_ref[hh] + cq)
            pt.append(jnp.exp2((st_ref[hh] - (m - cq)).astype(BF16)))
            alpha.append(jnp.exp2(m_old - m))
            m_scr[hh] = m
        pv = []
        for hh in heads:
            vt = vt_ref[0, hh * DH_C:(hh + 1) * DH_C, j * TK_C:(j + 1) * TK_C]
            pv.append(_dot(jnp.concatenate([vt, ones_v], axis=0), pt[hh]))
        for hh in heads:
            acc_scr[hh] = alpha[hh] * acc_scr[hh] + pv[hh]

    def finish(qi):
        rows = slice(qi * TQ_C, (qi + 1) * TQ_C)
        for pair in range(NH_C // 2):
            a0 = acc_scr[2 * pair]
            a1 = acc_scr[2 * pair + 1]
            ot = jnp.concatenate([a0[0:DH_C] / a0[DH_C:DH_C + 1], a1[0:DH_C] / a1[DH_C:DH_C + 1]], axis=0)
            cols = slice(pair * LANES, (pair + 1) * LANES)
            o_ref[0, rows, cols] = (ot.T * sg_ref[0, rows, cols].astype(F32)).astype(BF16)

    tiles = [(qi, j) for qi in range(SEQ // TQ_C) for j in range(qi + 1)]
    bufs = ((sta_scr, mxa_scr), (stb_scr, mxb_scr))
    def issue(n):
        qi, j = tiles[n]
        if j == 0:
            load_queries(qi)
        scores(qi, j, *bufs[n % 2])

    issue(0)
    issue(1)
    for n, (qi, j) in enumerate(tiles):
        if j == 0:
            acc_scr[...] = jnp.zeros(acc_scr.shape, F32)
            m_scr[...] = jnp.full(m_scr.shape, NEG_BIG, F32)
        consume(qi, j, *bufs[n % 2])
        if j == qi:
            finish(qi)
        if n + 2 < len(tiles):
            issue(n + 2)


def _fox_attention(qt, k, ek, vt, sg, cum):
    assert TQ_C == TK_C
    ng = H_C // NH_C
    w = NH_C * DH_C
    cq = cum.transpose(0, 2, 1).reshape(BATCH, ng, NH_C, SEQ)
    blk = pl.BlockSpec((1, SEQ, w), lambda b, g: (b, 0, g))
    blk_t = pl.BlockSpec((1, w, SEQ), lambda b, g: (b, g, 0))
    return pl.pallas_call(
        _attn_kernel,
        grid=(BATCH, ng),
        in_specs=[
            blk_t, blk, blk, blk_t, blk,
            pl.BlockSpec((1, 1, NH_C, SEQ), lambda b, g: (b, g, 0, 0)),
        ],
        out_specs=blk,
        out_shape=jax.ShapeDtypeStruct((BATCH, SEQ, E_C), BF16),
        scratch_shapes=[pltpu.VMEM((NH_C, 2 * LANES, TQ_C), BF16), pltpu.VMEM((NH_C, DH_C + DEN_ROWS, TQ_C), F32),
                        pltpu.VMEM((NH_C, 1, TQ_C), F32),
                        pltpu.VMEM((NH_C, TK_C, TQ_C), F32), pltpu.VMEM((NH_C, TK_C, TQ_C), F32),
                        pltpu.VMEM((NH_C, 1, TQ_C), F32), pltpu.VMEM((NH_C, 1, TQ_C), F32)],
        compiler_params=pltpu.CompilerParams(dimension_semantics=("arbitrary", "arbitrary"),
                                             vmem_limit_bytes=VMEM_LIMIT),
        name="fox_attention",
    )(qt, k, ek, vt, sg, cq)


def _fox_mixer(x3, norm, w_in, b_f, q_norm, k_norm):
    qt, k, ek, vt, sg, cum = _fox_proj(x3, norm, w_in, b_f, q_norm, k_norm)
    return _fox_attention(qt, k, ek, vt, sg, cum)


def kernel(x, l0_norm, l0_w_in, l0_v_norm, l0_w_s, l0_b_s, l0_w_out, l1_norm, l1_w_in, l1_conv_w, l1_conv_b, l1_w_a, l1_b_a, l1_w_x, l1_b_x, l1_lam, l1_w_out, l2_norm, l2_w_in, l2_b_f, l2_q_norm, l2_k_norm, l2_w_out, l3_norm, l3_w_in, l3_v_norm, l3_w_s, l3_b_s, l3_w_out):
    n = BATCH * SEQ
    x = _gmlp_layer(x.reshape(n, D_MODEL), l0_norm, l0_w_in, l0_v_norm, l0_w_s, l0_b_s, l0_w_out)
    x = _rglru_layer(x.reshape(BATCH, SEQ, D_MODEL), l1_norm, l1_w_in, l1_conv_w, l1_conv_b, l1_w_a, l1_b_a,
                     l1_w_x, l1_b_x, l1_lam, l1_w_out)
    y = _fox_mixer(x, l2_norm, l2_w_in, l2_b_f, l2_q_norm, l2_k_norm)
    x = _gmlp_layer(x.reshape(n, D_MODEL), l3_norm, l3_w_in, l3_v_norm, l3_w_s, l3_b_s, l3_w_out,
                    pending=(y.reshape(n, E_C), l2_w_out))
    return x.reshape(BATCH, SEQ, D_MODEL)
```

```python
import functools

import jax
import jax.numpy as jnp
from jax import lax
from jax.experimental import pallas as pl
from jax.experimental.pallas import tpu as pltpu

D_MODEL = 1024
BATCH = 8
SEQ = 2048
EPS = 1e-6
CHUNK = 128
E_A = 2 * D_MODEL
G_A = 8
DG_A = E_A // G_A
E_B = 3 * D_MODEL // 2
H_B = 16
BD_B = E_B // H_B
CONV_W = 4
LRU_C = 8.0
H_C = 16
DH_C = D_MODEL // H_C
E_C = H_C * DH_C

LANES = 128
VMEM_LIMIT = 56 * 1024 * 1024

TM_A = 512
WCH_A = 512
T_B = 64
HG_B = 4
GW_B = HG_B * BD_B
TM_C = 1024
TQ_C = 256
TK_C = 256
NH_C = 4
DEN_ROWS = 16
NEG_BIG = -1e30
LOG2E = 1.4426950408889634

F32 = jnp.float32
BF16 = jnp.bfloat16


def _dot(a, b):
    return jnp.dot(a, b, preferred_element_type=F32)


def _rms(x, g):
    ms = jnp.mean(x * x, axis=-1, keepdims=True)
    return x * lax.rsqrt(ms + EPS) * g


GELU_C = 0.7978845608028654


def _gelu_of_half(hx):
    return hx * (1.0 + jnp.tanh(hx * (2.0 * GELU_C + (8.0 * GELU_C * 0.044715) * (hx * hx))))


def _silu_of_half(hx):
    return hx * (1.0 + jnp.tanh(hx))


def _softplus(x):
    return jnp.maximum(x, 0.0) + jnp.log1p(jnp.exp(-jnp.abs(x)))


def _const_spec(shape):
    n = len(shape)
    return pl.BlockSpec(shape, lambda *_: (0,) * n, pipeline_mode=pl.Buffered(1))


def _win_chunk_copy(win_hbm, stage_scr, sem, c):
    cols = pl.ds(c * WCH_A, WCH_A)
    return pltpu.make_async_copy(win_hbm.at[:, cols], stage_scr.at[c % 2], sem.at[c % 2])


def _gmlp_kernel(has_pending, x_ref, *refs):
    if has_pending:
        yp_ref, wp_ref, *refs = refs
    nrm_ref, win_hbm, vnrm_ref, ws_ref, bst_ref, wo_ref, o_ref, vn_scr, y_scr, win_ref, stage_scr, sem = refs

    @pl.when(pl.program_id(0) == 0)
    def _():
        n_chunks = 3 * E_A // WCH_A
        _win_chunk_copy(win_hbm, stage_scr, sem, 0).start()
        for c in range(n_chunks):
            if c + 1 < n_chunks:
                _win_chunk_copy(win_hbm, stage_scr, sem, c + 1).start()
            _win_chunk_copy(win_hbm, stage_scr, sem, c).wait()
            win_ref[:, c * WCH_A:(c + 1) * WCH_A] = (0.5 * stage_scr[c % 2]).astype(BF16)

    if has_pending:
        x = x_ref[...] + _dot(yp_ref[...], wp_ref[...])
    else:
        x = x_ref[...]
    h = _rms(x, nrm_ref[...]).astype(BF16)
    v = _gelu_of_half(_dot(h, win_ref[:, E_A:2 * E_A]))
    vn_scr[...] = _rms(v, vnrm_ref[...]).astype(BF16)
    row = lax.broadcasted_iota(jnp.int32, (CHUNK, CHUNK), 0)
    col = lax.broadcasted_iota(jnp.int32, (CHUNK, CHUNK), 1)
    causal = col <= row
    for g in range(G_A):
        c0 = g * DG_A
        u = _gelu_of_half(_dot(h, win_ref[:, c0:c0 + DG_A]))
        gate = _silu_of_half(_dot(h, win_ref[:, 2 * E_A + c0:2 * E_A + c0 + DG_A]))
        w = jnp.where(causal, ws_ref[g], 0.0).astype(BF16)
        bias = bst_ref[:, g:g + 1]
        for c in range(TM_A // CHUNK):
            r0 = c * CHUNK
            mixed = _dot(w, vn_scr[r0:r0 + CHUNK, c0:c0 + DG_A]) + bias
            y = u[r0:r0 + CHUNK] * mixed * gate[r0:r0 + CHUNK]
            y_scr[r0:r0 + CHUNK, c0:c0 + DG_A] = y.astype(BF16)
    o_ref[...] = x + _dot(y_scr[...], wo_ref[...])


def _gmlp_layer(x2, norm, w_in, v_norm, w_s, b_s, w_out, pending=None):
    n = x2.shape[0]
    rows = pl.BlockSpec((TM_A, D_MODEL), lambda i: (i, 0))
    pending_specs, pending_args = [], []
    if pending is not None:
        y_prev, w_prev = pending
        pending_specs = [pl.BlockSpec((TM_A, y_prev.shape[1]), lambda i: (i, 0)), _const_spec(w_prev.shape)]
        pending_args = [y_prev, w_prev.astype(BF16)]
    return pl.pallas_call(
        functools.partial(_gmlp_kernel, pending is not None),
        grid=(n // TM_A,),
        in_specs=[
            rows,
            *pending_specs,
            _const_spec((1, D_MODEL)),
            pl.BlockSpec(memory_space=pl.ANY),
            _const_spec((1, E_A)),
            _const_spec((G_A, CHUNK, CHUNK)),
            _const_spec((CHUNK, G_A)),
            _const_spec((E_A, D_MODEL)),
        ],
        out_specs=rows,
        out_shape=jax.ShapeDtypeStruct((n, D_MODEL), F32),
        scratch_shapes=[pltpu.VMEM((TM_A, E_A), BF16), pltpu.VMEM((TM_A, E_A), BF16),
                        pltpu.VMEM((D_MODEL, 3 * E_A), BF16), pltpu.VMEM((2, D_MODEL, WCH_A), F32),
                        pltpu.SemaphoreType.DMA((2,))],
        compiler_params=pltpu.CompilerParams(dimension_semantics=("arbitrary",), vmem_limit_bytes=VMEM_LIMIT),
        name="gmlp_layer",
    )(x2, *pending_args, norm.reshape(1, D_MODEL), w_in, v_norm.reshape(1, E_A), w_s, b_s.T, w_out.astype(BF16))


ROWS_B = T_B * BATCH
TAIL_B = (CONV_W - 1) * BATCH


def _rglru_kernel(x_ref, nrm_ref, win_ref, cw_ref, cb_ref, wbd_ref, hba_ref, hbx_ref, lam_ref, wo_ref, o_ref,
                  hn_scr, xb_scr, a_scr, hs_scr, state_scr, out_scr):
    @pl.when(pl.program_id(0) == 0)
    def _():
        xb_scr[0:TAIL_B, :] = jnp.zeros((TAIL_B, E_B), F32)
        state_scr[...] = jnp.zeros((BATCH, E_B), F32)

    nrm = nrm_ref[...]
    for b in range(BATCH):
        hb = _rms(x_ref[b], nrm)
        for c in range(D_MODEL // LANES):
            hn_scr[c, pl.ds(b, T_B, stride=BATCH), :] = hb[:, c * LANES:(c + 1) * LANES]
    h = jnp.concatenate([hn_scr[c] for c in range(D_MODEL // LANES)], axis=1).astype(BF16)
    xb_scr[TAIL_B:TAIL_B + ROWS_B, :] = _dot(h, win_ref[:, 0:E_B])
    gate = _silu_of_half(_dot(h, win_ref[:, E_B:2 * E_B]))

    xc = cb_ref[...] + cw_ref[0:1, :] * xb_scr[0:ROWS_B, :]
    for k in range(1, CONV_W):
        xc = xc + cw_ref[k:k + 1, :] * xb_scr[k * BATCH:k * BATCH + ROWS_B, :]
    tail = xb_scr[ROWS_B:ROWS_B + TAIL_B, :]
    xb_scr[0:TAIL_B, :] = tail
    xcb = xc.astype(BF16)

    quarter_c_sp = (0.25 * LRU_C) * _softplus(-lam_ref[...])
    for j in range(E_B // GW_B):
        c0 = j * GW_B
        pre = _dot(xcb[:, c0:c0 + GW_B], wbd_ref[j])
        ur = jnp.tanh(pre[:, 0:GW_B] + hba_ref[:, c0:c0 + GW_B])
        ui = jnp.tanh(pre[:, GW_B:2 * GW_B] + hbx_ref[:, c0:c0 + GW_B])
        tn = jnp.tanh(quarter_c_sp[:, c0:c0 + GW_B] * (1.0 + ur))
        d = 1.0 / (1.0 + tn)
        a_scr[:, c0:c0 + GW_B] = (1.0 - tn) * d
        root = jnp.where(tn <= 0.0, 0.0, tn * lax.rsqrt(tn))
        hs_scr[:, c0:c0 + GW_B] = (d * root) * ((1.0 + ui) * xc[:, c0:c0 + GW_B])

    hcur = state_scr[...]
    for t in range(T_B):
        rows = slice(t * BATCH, (t + 1) * BATCH)
        hcur = a_scr[rows, :] * hcur + hs_scr[rows, :]
        hs_scr[rows, :] = hcur
    state_scr[...] = hcur

    y = (hs_scr[...] * gate).astype(BF16)
    out = _dot(y, wo_ref[...])
    for c in range(D_MODEL // LANES):
        out_scr[c] = out[:, c * LANES:(c + 1) * LANES]
    for b in range(BATCH):
        ob = jnp.concatenate([out_scr[c, pl.ds(b, T_B, stride=BATCH), :] for c in range(D_MODEL // LANES)], axis=1)
        o_ref[b] = x_ref[b] + ob


def _block_diag_gates(w_a, w_x):
    def bd(w):
        wg = w.reshape(H_B // HG_B, HG_B, BD_B, BD_B)
        eye = jnp.eye(HG_B, dtype=w.dtype)
        return jnp.einsum('ghij,hk->ghikj', wg, eye).reshape(H_B // HG_B, GW_B, GW_B)
    return jnp.concatenate([bd(w_a), bd(w_x)], axis=-1)


def _rglru_layer(x3, norm, w_in, conv_w, conv_b, w_a, b_a, w_x, b_x, lam, w_out):
    ng = H_B // HG_B
    w_half = jnp.concatenate([w_in[:, 0:E_B], 0.5 * w_in[:, E_B:2 * E_B]], axis=1)
    return pl.pallas_call(
        _rglru_kernel,
        grid=(SEQ // T_B,),
        in_specs=[
            pl.BlockSpec((BATCH, T_B, D_MODEL), lambda i: (0, i, 0)),
            _const_spec((1, D_MODEL)),
            _const_spec((D_MODEL, 2 * E_B)),
            _const_spec((CONV_W, E_B)),
            _const_spec((1, E_B)),
            _const_spec((ng, GW_B, 2 * GW_B)),
            _const_spec((1, E_B)),
            _const_spec((1, E_B)),
            _const_spec((1, E_B)),
            _const_spec((E_B, D_MODEL)),
        ],
        out_specs=pl.BlockSpec((BATCH, T_B, D_MODEL), lambda i: (0, i, 0)),
        out_shape=jax.ShapeDtypeStruct((BATCH, SEQ, D_MODEL), F32),
        scratch_shapes=[
            pltpu.VMEM((D_MODEL // LANES, ROWS_B, LANES), F32),
            pltpu.VMEM((TAIL_B + ROWS_B, E_B), F32),
            pltpu.VMEM((ROWS_B, E_B), F32),
            pltpu.VMEM((ROWS_B, E_B), F32),
            pltpu.VMEM((BATCH, E_B), F32),
            pltpu.VMEM((D_MODEL // LANES, ROWS_B, LANES), F32),
        ],
        compiler_params=pltpu.CompilerParams(dimension_semantics=("arbitrary",), vmem_limit_bytes=VMEM_LIMIT),
        name="rglru_layer",
    )(x3, norm.reshape(1, D_MODEL), w_half.astype(BF16), conv_w, conv_b.reshape(1, E_B),
      (0.5 * _block_diag_gates(w_a, w_x)).astype(BF16), (0.5 * b_a).reshape(1, E_B), (0.5 * b_x).reshape(1, E_B),
      lam.reshape(1, E_B), w_out.astype(BF16))


N_SPLIT = 3


def _fox_proj_kernel(x_ref, nrm_ref, wkg_ref, wqvt_ref, wf_ref, bf_ref, kg_ref, seg_ref, segt_ref, place_ref, tri_ref,
                     qt_ref, k_ref, ek_ref, vt_ref, sg_ref, cum_ref, carry_scr):
    @pl.when(pl.program_id(1) == 0)
    def _():
        carry_scr[...] = jnp.zeros((1, LANES), F32)

    h = _rms(x_ref[0], nrm_ref[...]).astype(BF16)

    zk = _dot(h, wkg_ref[:, 0:E_C])
    ss = _dot((zk * zk).astype(BF16), seg_ref[...])
    r = lax.rsqrt(ss * (1.0 / DH_C) + EPS)
    r_hi = r.astype(BF16)
    r_lo = (r - r_hi.astype(F32)).astype(BF16)
    rexp = _dot(jnp.concatenate([r_hi, r_lo], axis=1), segt_ref[...])
    k_ref[0] = (zk * rexp * kg_ref[...]).astype(BF16)
    sg_ref[0] = _silu_of_half(_dot(h, wkg_ref[:, E_C:2 * E_C])).astype(BF16)

    zqv = lax.dot_general(wqvt_ref[...], h, (((1,), (1,)), ((), ())), preferred_element_type=F32)
    vt_ref[0] = zqv[E_C:2 * E_C].astype(BF16)
    for hd in range(H_C):
        rows = slice(hd * DH_C, (hd + 1) * DH_C)
        zq = zqv[rows]
        rq = lax.rsqrt(jnp.mean(zq * zq, axis=0, keepdims=True) + EPS)
        qt_ref[0, rows, :] = (zq * rq).astype(BF16)

    z = _dot(h, wf_ref[...]) + bf_ref[...]
    log_f = jnp.minimum(z, 0.0) - jnp.log1p(jnp.exp(-jnp.abs(z)))
    group = lax.broadcasted_iota(jnp.int32, (TM_C, LANES), 1) // H_C

    def split3(v):
        hi = v.astype(BF16).astype(F32)
        mid = (v - hi).astype(BF16).astype(F32)
        lo = v - hi - mid
        return jnp.where(group == 0, hi, jnp.where(group == 1, mid, lo)).astype(BF16)

    cp = _dot(tri_ref[...], split3(log_f))
    cum = cp + pltpu.roll(cp, LANES - H_C, axis=1) + pltpu.roll(cp, LANES - 2 * H_C, axis=1) + carry_scr[...]
    carry_scr[...] = cum[TM_C - 1:TM_C, :]
    c2 = cum * LOG2E
    cum_ref[0] = c2[:, 0:H_C]
    c2 = jnp.where(group == 0, c2,
                   jnp.where(group == 1, pltpu.roll(c2, H_C, axis=1), pltpu.roll(c2, 2 * H_C, axis=1)))
    ek_ref[0] = _dot(split3(c2), place_ref[...]).astype(BF16)


def _fox_proj(x3, norm, w_in, b_f, q_norm, k_norm):
    w_kg = jnp.concatenate([w_in[:, E_C:2 * E_C], 0.5 * w_in[:, 3 * E_C:4 * E_C]], axis=1).astype(BF16)
    w_qvt = jnp.concatenate([w_in[:, 0:E_C], w_in[:, 2 * E_C:3 * E_C]], axis=1).T.astype(BF16)
    pad = LANES - N_SPLIT * H_C
    w_f = jnp.pad(jnp.tile(w_in[:, 4 * E_C:], (1, N_SPLIT)), ((0, 0), (0, pad))).astype(BF16)
    b_fp = jnp.pad(jnp.tile(b_f, N_SPLIT), (0, pad)).reshape(1, LANES)
    kg = jnp.tile(k_norm * q_norm, H_C).reshape(1, E_C) * (DH_C ** -0.5 * LOG2E)
    head_of = jnp.arange(E_C) // DH_C
    seg = (head_of[:, None] == jnp.arange(LANES)[None, :]).astype(BF16)
    segt2 = jnp.concatenate([seg.T, seg.T], axis=0)
    src = jnp.arange(LANES)
    dst = (src % H_C) * DH_C + src // H_C
    place = jnp.where((src[:, None] < N_SPLIT * H_C) & (dst[:, None] == jnp.arange(E_C)[None, :]), -1.0, 0.0)
    tok = jax.ShapeDtypeStruct((BATCH, SEQ, E_C), BF16)
    tok_t = jax.ShapeDtypeStruct((BATCH, E_C, SEQ), BF16)
    blk = pl.BlockSpec((1, TM_C, E_C), lambda b, i: (b, i, 0))
    blk_t = pl.BlockSpec((1, E_C, TM_C), lambda b, i: (b, 0, i))
    return pl.pallas_call(
        _fox_proj_kernel,
        grid=(BATCH, SEQ // TM_C),
        in_specs=[
            pl.BlockSpec((1, TM_C, D_MODEL), lambda b, i: (b, i, 0)),
            _const_spec((1, D_MODEL)),
            _const_spec((D_MODEL, 2 * E_C)),
            _const_spec((2 * E_C, D_MODEL)),
            _const_spec((D_MODEL, LANES)),
            _const_spec((1, LANES)),
            _const_spec((1, E_C)),
            _const_spec((E_C, LANES)),
            _const_spec((2 * LANES, E_C)),
            _const_spec((LANES, E_C)),
            _const_spec((TM_C, TM_C)),
        ],
        out_specs=[blk_t, blk, blk, blk_t, blk, pl.BlockSpec((1, TM_C, H_C), lambda b, i: (b, i, 0))],
        out_shape=[tok_t, tok, tok, tok_t, tok, jax.ShapeDtypeStruct((BATCH, SEQ, H_C), F32)],
        scratch_shapes=[pltpu.VMEM((1, LANES), F32)],
        compiler_params=pltpu.CompilerParams(dimension_semantics=("arbitrary", "arbitrary"),
                                             vmem_limit_bytes=VMEM_LIMIT),
        name="fox_proj",
    )(x3, norm.reshape(1, D_MODEL), w_kg, w_qvt, w_f, b_fp, kg, seg, segt2, place.astype(BF16),
      jnp.tril(jnp.ones((TM_C, TM_C), BF16)))


def _attn_kernel(qt_ref, k_ref, ek_ref, vt_ref, sg_ref, cq_ref, o_ref,
                 qa_scr, acc_scr, m_scr, sta_scr, stb_scr, mxa_scr, mxb_scr):
    causal = (lax.broadcasted_iota(jnp.int32, (TK_C, TQ_C), 0) <= lax.broadcasted_iota(jnp.int32, (TK_C, TQ_C), 1))
    ones_v = jnp.ones((DEN_ROWS, TK_C), BF16)
    zero_q = jnp.zeros((DH_C, TQ_C), BF16)
    row_q = lax.broadcasted_iota(jnp.int32, (LANES, TQ_C), 0)
    heads = range(NH_C)

    for hh in heads:
        odd = hh % 2
        qa_scr[hh, (1 - odd) * DH_C:(2 - odd) * DH_C, :] = zero_q
        mine = jnp.logical_and(row_q >= odd * DH_C, row_q < odd * DH_C + N_SPLIT)
        qa_scr[hh, LANES:2 * LANES, :] = jnp.where(mine, 1.0, 0.0).astype(BF16)

    def load_queries(qi):
        cols = slice(qi * TQ_C, (qi + 1) * TQ_C)
        for hh in heads:
            odd = hh % 2
            qa_scr[hh, odd * DH_C:(odd + 1) * DH_C, :] = qt_ref[0, hh * DH_C:(hh + 1) * DH_C, cols]

    def scores(qi, j, st_ref, mx_ref):
        keys = slice(j * TK_C, (j + 1) * TK_C)
        for hh in heads:
            pair = hh // 2
            ka = jnp.concatenate([k_ref[0, keys, pair * LANES:(pair + 1) * LANES],
                                  ek_ref[0, keys, pair * LANES:(pair + 1) * LANES]], axis=1)
            st = _dot(ka, qa_scr[hh])
            if j == qi:
                st = jnp.where(causal, st, NEG_BIG)
            st_ref[hh] = st
            mx_ref[hh] = jnp.max(st, axis=0, keepdims=True)

    def consume(qi, j, st_ref, mx_ref):
        alpha, pt = [], []
        for hh in heads:
            cq = cq_ref[0, 0, hh:hh + 1, qi * TQ_C:(qi + 1) * TQ_C]
            if j == 0:
                m = mx_ref[hh] + cq
            else:
                m_old = m_scr[hh]
                m = jnp.maximum(m_old, mx_ref[hh] + cq)
                alpha.append(jnp.exp2(m_old - m))
            pt.append(jnp.exp2((st_ref[hh] - (m - cq)).astype(BF16)))
            m_scr[hh] = m
        pv = []
        for hh in heads:
            vt = vt_ref[0, hh * DH_C:(hh + 1) * DH_C, j * TK_C:(j + 1) * TK_C]
            pv.append(_dot(jnp.concatenate([vt, ones_v], axis=0), pt[hh]))
        for hh in heads:
            acc_scr[hh] = pv[hh] if j == 0 else alpha[hh] * acc_scr[hh] + pv[hh]

    def finish(qi):
        rows = slice(qi * TQ_C, (qi + 1) * TQ_C)
        for pair in range(NH_C // 2):
            a0 = acc_scr[2 * pair]
            a1 = acc_scr[2 * pair + 1]
            ot = jnp.concatenate([a0[0:DH_C] / a0[DH_C:DH_C + 1], a1[0:DH_C] / a1[DH_C:DH_C + 1]], axis=0)
            cols = slice(pair * LANES, (pair + 1) * LANES)
            o_ref[0, rows, cols] = (ot.T * sg_ref[0, rows, cols].astype(F32)).astype(BF16)

    tiles = [(qi, j) for qi in range(SEQ // TQ_C) for j in range(qi + 1)]
    bufs = ((sta_scr, mxa_scr), (stb_scr, mxb_scr))
    def issue(n):
        qi, j = tiles[n]
        if j == 0:
            load_queries(qi)
        scores(qi, j, *bufs[n % 2])

    issue(0)
    issue(1)
    for n, (qi, j) in enumerate(tiles):
        consume(qi, j, *bufs[n % 2])
        if j == qi:
            finish(qi)
        if n + 2 < len(tiles):
            issue(n + 2)


def _fox_attention(qt, k, ek, vt, sg, cum):
    assert TQ_C == TK_C
    ng = H_C // NH_C
    w = NH_C * DH_C
    cq = cum.transpose(0, 2, 1).reshape(BATCH, ng, NH_C, SEQ)
    blk = pl.BlockSpec((1, SEQ, w), lambda b, g: (b, 0, g))
    blk_t = pl.BlockSpec((1, w, SEQ), lambda b, g: (b, g, 0))
    return pl.pallas_call(
        _attn_kernel,
        grid=(BATCH, ng),
        in_specs=[
            blk_t, blk, blk, blk_t, blk,
            pl.BlockSpec((1, 1, NH_C, SEQ), lambda b, g: (b, g, 0, 0)),
        ],
        out_specs=blk,
        out_shape=jax.ShapeDtypeStruct((BATCH, SEQ, E_C), BF16),
        scratch_shapes=[pltpu.VMEM((NH_C, 2 * LANES, TQ_C), BF16), pltpu.VMEM((NH_C, DH_C + DEN_ROWS, TQ_C), F32),
                        pltpu.VMEM((NH_C, 1, TQ_C), F32),
                        pltpu.VMEM((NH_C, TK_C, TQ_C), F32), pltpu.VMEM((NH_C, TK_C, TQ_C), F32),
                        pltpu.VMEM((NH_C, 1, TQ_C), F32), pltpu.VMEM((NH_C, 1, TQ_C), F32)],
        compiler_params=pltpu.CompilerParams(dimension_semantics=("arbitrary", "arbitrary"),
                                             vmem_limit_bytes=VMEM_LIMIT),
        name="fox_attention",
    )(qt, k, ek, vt, sg, cq)


def _fox_mixer(x3, norm, w_in, b_f, q_norm, k_norm):
    qt, k, ek, vt, sg, cum = _fox_proj(x3, norm, w_in, b_f, q_norm, k_norm)
    return _fox_attention(qt, k, ek, vt, sg, cum)


def kernel(x, l0_norm, l0_w_in, l0_v_norm, l0_w_s, l0_b_s, l0_w_out, l1_norm, l1_w_in, l1_conv_w, l1_conv_b, l1_w_a, l1_b_a, l1_w_x, l1_b_x, l1_lam, l1_w_out, l2_norm, l2_w_in, l2_b_f, l2_q_norm, l2_k_norm, l2_w_out, l3_norm, l3_w_in, l3_v_norm, l3_w_s, l3_b_s, l3_w_out):
    n = BATCH * SEQ
    x = _gmlp_layer(x.reshape(n, D_MODEL), l0_norm, l0_w_in, l0_v_norm, l0_w_s, l0_b_s, l0_w_out)
    x = _rglru_layer(x.reshape(BATCH, SEQ, D_MODEL), l1_norm, l1_w_in, l1_conv_w, l1_conv_b, l1_w_a, l1_b_a,
                     l1_w_x, l1_b_x, l1_lam, l1_w_out)
    y = _fox_mixer(x, l2_norm, l2_w_in, l2_b_f, l2_q_norm, l2_k_norm)
    x = _gmlp_layer(x.reshape(n, D_MODEL), l3_norm, l3_w_in, l3_v_norm, l3_w_s, l3_b_s, l3_w_out,
                    pending=(y.reshape(n, E_C), l2_w_out))
    return x.reshape(BATCH, SEQ, D_MODEL)
```

```python
import functools

import jax
import jax.numpy as jnp
from jax import lax
from jax.experimental import pallas as pl
from jax.experimental.pallas import tpu as pltpu

D_MODEL = 1024
BATCH = 8
SEQ = 2048
EPS = 1e-6
CHUNK = 128
E_A = 2 * D_MODEL
G_A = 8
DG_A = E_A // G_A
E_B = 3 * D_MODEL // 2
H_B = 16
BD_B = E_B // H_B
CONV_W = 4
LRU_C = 8.0
H_C = 16
DH_C = D_MODEL // H_C
E_C = H_C * DH_C

LANES = 128
VMEM_LIMIT = 56 * 1024 * 1024

TM_A = 512
WCH_A = 1024
T_B = 64
HG_B = 4
GW_B = HG_B * BD_B
TM_C = 1024
TQ_C = 256
TK_C = 256
NH_C = 4
DEN_ROWS = 16
NEG_BIG = -1e30
LOG2E = 1.4426950408889634

F32 = jnp.float32
BF16 = jnp.bfloat16


def _dot(a, b):
    return jnp.dot(a, b, preferred_element_type=F32)


def _rms(x, g):
    ms = jnp.mean(x * x, axis=-1, keepdims=True)
    return x * lax.rsqrt(ms + EPS) * g


GELU_C = 0.7978845608028654


def _gelu_of_half(hx):
    return hx * (1.0 + jnp.tanh(hx * (2.0 * GELU_C + (8.0 * GELU_C * 0.044715) * (hx * hx))))


def _silu_of_half(hx):
    return hx * (1.0 + jnp.tanh(hx))


def _softplus(x):
    return jnp.maximum(x, 0.0) + jnp.log1p(jnp.exp(-jnp.abs(x)))


def _const_spec(shape):
    n = len(shape)
    return pl.BlockSpec(shape, lambda *_: (0,) * n, pipeline_mode=pl.Buffered(1))


def _win_chunk_copy(win_hbm, stage_scr, sem, c):
    cols = pl.ds(c * WCH_A, WCH_A)
    return pltpu.make_async_copy(win_hbm.at[:, cols], stage_scr.at[c % 2], sem.at[c % 2])


def _gmlp_kernel(has_pending, x_ref, *refs):
    if has_pending:
        yp_ref, wp_ref, *refs = refs
    nrm_ref, win_hbm, vnrm_ref, ws_ref, bst_ref, wo_ref, o_ref, vn_scr, y_scr, win_ref, stage_scr, sem = refs

    @pl.when(pl.program_id(0) == 0)
    def _():
        n_chunks = 3 * E_A // WCH_A
        _win_chunk_copy(win_hbm, stage_scr, sem, 0).start()
        for c in range(n_chunks):
            if c + 1 < n_chunks:
                _win_chunk_copy(win_hbm, stage_scr, sem, c + 1).start()
            _win_chunk_copy(win_hbm, stage_scr, sem, c).wait()
            win_ref[:, c * WCH_A:(c + 1) * WCH_A] = (0.5 * stage_scr[c % 2]).astype(BF16)

    if has_pending:
        x = x_ref[...] + _dot(yp_ref[...], wp_ref[...])
    else:
        x = x_ref[...]
    h = _rms(x, nrm_ref[...]).astype(BF16)
    v = _gelu_of_half(_dot(h, win_ref[:, E_A:2 * E_A]))
    vn_scr[...] = _rms(v, vnrm_ref[...]).astype(BF16)
    row = lax.broadcasted_iota(jnp.int32, (CHUNK, CHUNK), 0)
    col = lax.broadcasted_iota(jnp.int32, (CHUNK, CHUNK), 1)
    causal = col <= row
    for g in range(G_A):
        c0 = g * DG_A
        u = _gelu_of_half(_dot(h, win_ref[:, c0:c0 + DG_A]))
        gate = _silu_of_half(_dot(h, win_ref[:, 2 * E_A + c0:2 * E_A + c0 + DG_A]))
        w = jnp.where(causal, ws_ref[g], 0.0).astype(BF16)
        bias = bst_ref[:, g:g + 1]
        for c in range(TM_A // CHUNK):
            r0 = c * CHUNK
            mixed = _dot(w, vn_scr[r0:r0 + CHUNK, c0:c0 + DG_A]) + bias
            y = u[r0:r0 + CHUNK] * mixed * gate[r0:r0 + CHUNK]
            y_scr[r0:r0 + CHUNK, c0:c0 + DG_A] = y.astype(BF16)
    o_ref[...] = x + _dot(y_scr[...], wo_ref[...])


def _gmlp_layer(x2, norm, w_in, v_norm, w_s, b_s, w_out, pending=None):
    n = x2.shape[0]
    rows = pl.BlockSpec((TM_A, D_MODEL), lambda i: (i, 0))
    pending_specs, pending_args = [], []
    if pending is not None:
        y_prev, w_prev = pending
        pending_specs = [pl.BlockSpec((TM_A, y_prev.shape[1]), lambda i: (i, 0)), _const_spec(w_prev.shape)]
        pending_args = [y_prev, w_prev.astype(BF16)]
    return pl.pallas_call(
        functools.partial(_gmlp_kernel, pending is not None),
        grid=(n // TM_A,),
        in_specs=[
            rows,
            *pending_specs,
            _const_spec((1, D_MODEL)),
            pl.BlockSpec(memory_space=pl.ANY),
            _const_spec((1, E_A)),
            _const_spec((G_A, CHUNK, CHUNK)),
            _const_spec((CHUNK, G_A)),
            _const_spec((E_A, D_MODEL)),
        ],
        out_specs=rows,
        out_shape=jax.ShapeDtypeStruct((n, D_MODEL), F32),
        scratch_shapes=[pltpu.VMEM((TM_A, E_A), BF16), pltpu.VMEM((TM_A, E_A), BF16),
                        pltpu.VMEM((D_MODEL, 3 * E_A), BF16), pltpu.VMEM((2, D_MODEL, WCH_A), F32),
                        pltpu.SemaphoreType.DMA((2,))],
        compiler_params=pltpu.CompilerParams(dimension_semantics=("arbitrary",), vmem_limit_bytes=VMEM_LIMIT),
        name="gmlp_layer",
    )(x2, *pending_args, norm.reshape(1, D_MODEL), w_in, v_norm.reshape(1, E_A), w_s, b_s.T, w_out.astype(BF16))


ROWS_B = T_B * BATCH
TAIL_B = (CONV_W - 1) * BATCH


def _rglru_kernel(x_ref, nrm_ref, win_ref, cw_ref, cb_ref, wbd_ref, hba_ref, hbx_ref, lam_ref, wo_ref, o_ref,
                  hn_scr, xb_scr, a_scr, hs_scr, state_scr, out_scr):
    @pl.when(pl.program_id(0) == 0)
    def _():
        xb_scr[0:TAIL_B, :] = jnp.zeros((TAIL_B, E_B), F32)
        state_scr[...] = jnp.zeros((BATCH, E_B), F32)

    nrm = nrm_ref[...]
    for b in range(BATCH):
        hb = _rms(x_ref[b], nrm)
        for c in range(D_MODEL // LANES):
            hn_scr[c, pl.ds(b, T_B, stride=BATCH), :] = hb[:, c * LANES:(c + 1) * LANES]
    h = jnp.concatenate([hn_scr[c] for c in range(D_MODEL // LANES)], axis=1).astype(BF16)
    xb_scr[TAIL_B:TAIL_B + ROWS_B, :] = _dot(h, win_ref[:, 0:E_B])
    gate = _silu_of_half(_dot(h, win_ref[:, E_B:2 * E_B]))

    xc = cb_ref[...] + cw_ref[0:1, :] * xb_scr[0:ROWS_B, :]
    for k in range(1, CONV_W):
        xc = xc + cw_ref[k:k + 1, :] * xb_scr[k * BATCH:k * BATCH + ROWS_B, :]
    tail = xb_scr[ROWS_B:ROWS_B + TAIL_B, :]
    xb_scr[0:TAIL_B, :] = tail
    xcb = xc.astype(BF16)

    quarter_c_sp = (0.25 * LRU_C) * _softplus(-lam_ref[...])
    for j in range(E_B // GW_B):
        c0 = j * GW_B
        pre = _dot(xcb[:, c0:c0 + GW_B], wbd_ref[j])
        ur = jnp.tanh(pre[:, 0:GW_B] + hba_ref[:, c0:c0 + GW_B])
        ui = jnp.tanh(pre[:, GW_B:2 * GW_B] + hbx_ref[:, c0:c0 + GW_B])
        tn = jnp.tanh(quarter_c_sp[:, c0:c0 + GW_B] * (1.0 + ur))
        d = 1.0 / (1.0 + tn)
        a_scr[:, c0:c0 + GW_B] = (1.0 - tn) * d
        root = jnp.where(tn <= 0.0, 0.0, tn * lax.rsqrt(tn))
        hs_scr[:, c0:c0 + GW_B] = (d * root) * ((1.0 + ui) * xc[:, c0:c0 + GW_B])

    hcur = state_scr[...]
    for t in range(T_B):
        rows = slice(t * BATCH, (t + 1) * BATCH)
        hcur = a_scr[rows, :] * hcur + hs_scr[rows, :]
        hs_scr[rows, :] = hcur
    state_scr[...] = hcur

    y = (hs_scr[...] * gate).astype(BF16)
    out = _dot(y, wo_ref[...])
    for c in range(D_MODEL // LANES):
        out_scr[c] = out[:, c * LANES:(c + 1) * LANES]
    for b in range(BATCH):
        ob = jnp.concatenate([out_scr[c, pl.ds(b, T_B, stride=BATCH), :] for c in range(D_MODEL // LANES)], axis=1)
        o_ref[b] = x_ref[b] + ob


def _block_diag_gates(w_a, w_x):
    def bd(w):
        wg = w.reshape(H_B // HG_B, HG_B, BD_B, BD_B)
        eye = jnp.eye(HG_B, dtype=w.dtype)
        return jnp.einsum('ghij,hk->ghikj', wg, eye).reshape(H_B // HG_B, GW_B, GW_B)
    return jnp.concatenate([bd(w_a), bd(w_x)], axis=-1)


def _rglru_layer(x3, norm, w_in, conv_w, conv_b, w_a, b_a, w_x, b_x, lam, w_out):
    ng = H_B // HG_B
    w_half = jnp.concatenate([w_in[:, 0:E_B], 0.5 * w_in[:, E_B:2 * E_B]], axis=1)
    return pl.pallas_call(
        _rglru_kernel,
        grid=(SEQ // T_B,),
        in_specs=[
            pl.BlockSpec((BATCH, T_B, D_MODEL), lambda i: (0, i, 0)),
            _const_spec((1, D_MODEL)),
            _const_spec((D_MODEL, 2 * E_B)),
            _const_spec((CONV_W, E_B)),
            _const_spec((1, E_B)),
            _const_spec((ng, GW_B, 2 * GW_B)),
            _const_spec((1, E_B)),
            _const_spec((1, E_B)),
            _const_spec((1, E_B)),
            _const_spec((E_B, D_MODEL)),
        ],
        out_specs=pl.BlockSpec((BATCH, T_B, D_MODEL), lambda i: (0, i, 0)),
        out_shape=jax.ShapeDtypeStruct((BATCH, SEQ, D_MODEL), F32),
        scratch_shapes=[
            pltpu.VMEM((D_MODEL // LANES, ROWS_B, LANES), F32),
            pltpu.VMEM((TAIL_B + ROWS_B, E_B), F32),
            pltpu.VMEM((ROWS_B, E_B), F32),
            pltpu.VMEM((ROWS_B, E_B), F32),
            pltpu.VMEM((BATCH, E_B), F32),
            pltpu.VMEM((D_MODEL // LANES, ROWS_B, LANES), F32),
        ],
        compiler_params=pltpu.CompilerParams(dimension_semantics=("arbitrary",), vmem_limit_bytes=VMEM_LIMIT),
        name="rglru_layer",
    )(x3, norm.reshape(1, D_MODEL), w_half.astype(BF16), conv_w, conv_b.reshape(1, E_B),
      (0.5 * _block_diag_gates(w_a, w_x)).astype(BF16), (0.5 * b_a).reshape(1, E_B), (0.5 * b_x).reshape(1, E_B),
      lam.reshape(1, E_B), w_out.astype(BF16))


N_SPLIT = 3


def _fox_proj_kernel(x_ref, nrm_ref, wkg_ref, wqvt_ref, wf_ref, bf_ref, kg_ref, seg_ref, segt_ref, place_ref, tri_ref,
                     qt_ref, k_ref, ek_ref, vt_ref, sg_ref, cum_ref, carry_scr):
    @pl.when(pl.program_id(1) == 0)
    def _():
        carry_scr[...] = jnp.zeros((1, LANES), F32)

    h = _rms(x_ref[0], nrm_ref[...]).astype(BF16)

    zk = _dot(h, wkg_ref[:, 0:E_C])
    ss = _dot((zk * zk).astype(BF16), seg_ref[...])
    r = lax.rsqrt(ss * (1.0 / DH_C) + EPS)
    r_hi = r.astype(BF16)
    r_lo = (r - r_hi.astype(F32)).astype(BF16)
    rexp = _dot(jnp.concatenate([r_hi, r_lo], axis=1), segt_ref[...])
    k_ref[0] = (zk * rexp * kg_ref[...]).astype(BF16)
    sg_ref[0] = _silu_of_half(_dot(h, wkg_ref[:, E_C:2 * E_C])).astype(BF16)

    zqv = lax.dot_general(wqvt_ref[...], h, (((1,), (1,)), ((), ())), preferred_element_type=F32)
    vt_ref[0] = zqv[E_C:2 * E_C].astype(BF16)
    for hd in range(H_C):
        rows = slice(hd * DH_C, (hd + 1) * DH_C)
        zq = zqv[rows]
        rq = lax.rsqrt(jnp.mean(zq * zq, axis=0, keepdims=True) + EPS)
        qt_ref[0, rows, :] = (zq * rq).astype(BF16)

    z = _dot(h, wf_ref[...]) + bf_ref[...]
    log_f = jnp.minimum(z, 0.0) - jnp.log1p(jnp.exp(-jnp.abs(z)))
    group = lax.broadcasted_iota(jnp.int32, (TM_C, LANES), 1) // H_C

    def split3(v):
        hi = v.astype(BF16).astype(F32)
        mid = (v - hi).astype(BF16).astype(F32)
        lo = v - hi - mid
        return jnp.where(group == 0, hi, jnp.where(group == 1, mid, lo)).astype(BF16)

    cp = _dot(tri_ref[...], split3(log_f))
    cum = cp + pltpu.roll(cp, LANES - H_C, axis=1) + pltpu.roll(cp, LANES - 2 * H_C, axis=1) + carry_scr[...]
    carry_scr[...] = cum[TM_C - 1:TM_C, :]
    c2 = cum * LOG2E
    cum_ref[0] = c2[:, 0:H_C]
    c2 = jnp.where(group == 0, c2,
                   jnp.where(group == 1, pltpu.roll(c2, H_C, axis=1), pltpu.roll(c2, 2 * H_C, axis=1)))
    ek_ref[0] = _dot(split3(c2), place_ref[...]).astype(BF16)


def _fox_proj(x3, norm, w_in, b_f, q_norm, k_norm):
    w_kg = jnp.concatenate([w_in[:, E_C:2 * E_C], 0.5 * w_in[:, 3 * E_C:4 * E_C]], axis=1).astype(BF16)
    w_qvt = jnp.concatenate([w_in[:, 0:E_C], w_in[:, 2 * E_C:3 * E_C]], axis=1).T.astype(BF16)
    pad = LANES - N_SPLIT * H_C
    w_f = jnp.pad(jnp.tile(w_in[:, 4 * E_C:], (1, N_SPLIT)), ((0, 0), (0, pad))).astype(BF16)
    b_fp = jnp.pad(jnp.tile(b_f, N_SPLIT), (0, pad)).reshape(1, LANES)
    kg = jnp.tile(k_norm * q_norm, H_C).reshape(1, E_C) * (DH_C ** -0.5 * LOG2E)
    head_of = jnp.arange(E_C) // DH_C
    seg = (head_of[:, None] == jnp.arange(LANES)[None, :]).astype(BF16)
    segt2 = jnp.concatenate([seg.T, seg.T], axis=0)
    src = jnp.arange(LANES)
    dst = (src % H_C) * DH_C + src // H_C
    place = jnp.where((src[:, None] < N_SPLIT * H_C) & (dst[:, None] == jnp.arange(E_C)[None, :]), -1.0, 0.0)
    tok = jax.ShapeDtypeStruct((BATCH, SEQ, E_C), BF16)
    tok_t = jax.ShapeDtypeStruct((BATCH, E_C, SEQ), BF16)
    blk = pl.BlockSpec((1, TM_C, E_C), lambda b, i: (b, i, 0))
    blk_t = pl.BlockSpec((1, E_C, TM_C), lambda b, i: (b, 0, i))
    return pl.pallas_call(
        _fox_proj_kernel,
        grid=(BATCH, SEQ // TM_C),
        in_specs=[
            pl.BlockSpec((1, TM_C, D_MODEL), lambda b, i: (b, i, 0)),
            _const_spec((1, D_MODEL)),
            _const_spec((D_MODEL, 2 * E_C)),
            _const_spec((2 * E_C, D_MODEL)),
            _const_spec((D_MODEL, LANES)),
            _const_spec((1, LANES)),
            _const_spec((1, E_C)),
            _const_spec((E_C, LANES)),
            _const_spec((2 * LANES, E_C)),
            _const_spec((LANES, E_C)),
            _const_spec((TM_C, TM_C)),
        ],
        out_specs=[blk_t, blk, blk, blk_t, blk, pl.BlockSpec((1, TM_C, H_C), lambda b, i: (b, i, 0))],
        out_shape=[tok_t, tok, tok, tok_t, tok, jax.ShapeDtypeStruct((BATCH, SEQ, H_C), F32)],
        scratch_shapes=[pltpu.VMEM((1, LANES), F32)],
        compiler_params=pltpu.CompilerParams(dimension_semantics=("arbitrary", "arbitrary"),
                                             vmem_limit_bytes=VMEM_LIMIT),
        name="fox_proj",
    )(x3, norm.reshape(1, D_MODEL), w_kg, w_qvt, w_f, b_fp, kg, seg, segt2, place.astype(BF16),
      jnp.tril(jnp.ones((TM_C, TM_C), BF16)))


def _attn_kernel(qt_ref, k_ref, ek_ref, vt_ref, sg_ref, cq_ref, o_ref,
                 qa_scr, acc_scr, m_scr, sta_scr, stb_scr, mxa_scr, mxb_scr):
    causal = (lax.broadcasted_iota(jnp.int32, (TK_C, TQ_C), 0) <= lax.broadcasted_iota(jnp.int32, (TK_C, TQ_C), 1))
    ones_v = jnp.ones((DEN_ROWS, TK_C), BF16)
    zero_q = jnp.zeros((DH_C, TQ_C), BF16)
    row_q = lax.broadcasted_iota(jnp.int32, (LANES, TQ_C), 0)
    heads = range(NH_C)

    for hh in heads:
        odd = hh % 2
        qa_scr[hh, (1 - odd) * DH_C:(2 - odd) * DH_C, :] = zero_q
        mine = jnp.logical_and(row_q >= odd * DH_C, row_q < odd * DH_C + N_SPLIT)
        qa_scr[hh, LANES:2 * LANES, :] = jnp.where(mine, 1.0, 0.0).astype(BF16)

    def load_queries(qi):
        cols = slice(qi * TQ_C, (qi + 1) * TQ_C)
        for hh in heads:
            odd = hh % 2
            qa_scr[hh, odd * DH_C:(odd + 1) * DH_C, :] = qt_ref[0, hh * DH_C:(hh + 1) * DH_C, cols]

    def scores(qi, j, st_ref, mx_ref):
        keys = slice(j * TK_C, (j + 1) * TK_C)
        for hh in heads:
            pair = hh // 2
            ka = jnp.concatenate([k_ref[0, keys, pair * LANES:(pair + 1) * LANES],
                                  ek_ref[0, keys, pair * LANES:(pair + 1) * LANES]], axis=1)
            st = _dot(ka, qa_scr[hh])
            if j == qi:
                st = jnp.where(causal, st, NEG_BIG)
            st_ref[hh] = st
            mx_ref[hh] = jnp.max(st, axis=0, keepdims=True)

    def consume(qi, j, st_ref, mx_ref):
        alpha, pt = [], []
        for hh in heads:
            cq = cq_ref[0, 0, hh:hh + 1, qi * TQ_C:(qi + 1) * TQ_C]
            if j == 0:
                m = mx_ref[hh] + cq
            else:
                m_old = m_scr[hh]
                m = jnp.maximum(m_old, mx_ref[hh] + cq)
                alpha.append(jnp.exp2(m_old - m))
            pt.append(jnp.exp2((st_ref[hh] - (m - cq)).astype(BF16)))
            m_scr[hh] = m
        pv = []
        for hh in heads:
            vt = vt_ref[0, hh * DH_C:(hh + 1) * DH_C, j * TK_C:(j + 1) * TK_C]
            pv.append(_dot(jnp.concatenate([vt, ones_v], axis=0), pt[hh]))
        for hh in heads:
            acc_scr[hh] = pv[hh] if j == 0 else alpha[hh] * acc_scr[hh] + pv[hh]

    def finish(qi):
        rows = slice(qi * TQ_C, (qi + 1) * TQ_C)
        for pair in range(NH_C // 2):
            a0 = acc_scr[2 * pair]
            a1 = acc_scr[2 * pair + 1]
            ot = jnp.concatenate([a0[0:DH_C] / a0[DH_C:DH_C + 1], a1[0:DH_C] / a1[DH_C:DH_C + 1]], axis=0)
            cols = slice(pair * LANES, (pair + 1) * LANES)
            o_ref[0, rows, cols] = (ot.T * sg_ref[0, rows, cols].astype(F32)).astype(BF16)

    tiles = [(qi, j) for qi in range(SEQ // TQ_C) for j in range(qi + 1)]
    bufs = ((sta_scr, mxa_scr), (stb_scr, mxb_scr))
    def issue(n):
        qi, j = tiles[n]
        if j == 0:
            load_queries(qi)
        scores(qi, j, *bufs[n % 2])

    issue(0)
    issue(1)
    for n, (qi, j) in enumerate(tiles):
        consume(qi, j, *bufs[n % 2])
        if j == qi:
            finish(qi)
        if n + 2 < len(tiles):
            issue(n + 2)


def _fox_attention(qt, k, ek, vt, sg, cum):
    assert TQ_C == TK_C
    ng = H_C // NH_C
    w = NH_C * DH_C
    cq = cum.transpose(0, 2, 1).reshape(BATCH, ng, NH_C, SEQ)
    blk = pl.BlockSpec((1, SEQ, w), lambda b, g: (b, 0, g))
    blk_t = pl.BlockSpec((1, w, SEQ), lambda b, g: (b, g, 0))
    return pl.pallas_call(
        _attn_kernel,
        grid=(BATCH, ng),
        in_specs=[
            blk_t, blk, blk, blk_t, blk,
            pl.BlockSpec((1, 1, NH_C, SEQ), lambda b, g: (b, g, 0, 0)),
        ],
        out_specs=blk,
        out_shape=jax.ShapeDtypeStruct((BATCH, SEQ, E_C), BF16),
        scratch_shapes=[pltpu.VMEM((NH_C, 2 * LANES, TQ_C), BF16), pltpu.VMEM((NH_C, DH_C + DEN_ROWS, TQ_C), F32),
                        pltpu.VMEM((NH_C, 1, TQ_C), F32),
                        pltpu.VMEM((NH_C, TK_C, TQ_C), F32), pltpu.VMEM((NH_C, TK_C, TQ_C), F32),
                        pltpu.VMEM((NH_C, 1, TQ_C), F32), pltpu.VMEM((NH_C, 1, TQ_C), F32)],
        compiler_params=pltpu.CompilerParams(dimension_semantics=("arbitrary", "arbitrary"),
                                             vmem_limit_bytes=VMEM_LIMIT),
        name="fox_attention",
    )(qt, k, ek, vt, sg, cq)


def _fox_mixer(x3, norm, w_in, b_f, q_norm, k_norm):
    qt, k, ek, vt, sg, cum = _fox_proj(x3, norm, w_in, b_f, q_norm, k_norm)
    return _fox_attention(qt, k, ek, vt, sg, cum)


def kernel(x, l0_norm, l0_w_in, l0_v_norm, l0_w_s, l0_b_s, l0_w_out, l1_norm, l1_w_in, l1_conv_w, l1_conv_b, l1_w_a, l1_b_a, l1_w_x, l1_b_x, l1_lam, l1_w_out, l2_norm, l2_w_in, l2_b_f, l2_q_norm, l2_k_norm, l2_w_out, l3_norm, l3_w_in, l3_v_norm, l3_w_s, l3_b_s, l3_w_out):
    n = BATCH * SEQ
    x = _gmlp_layer(x.reshape(n, D_MODEL), l0_norm, l0_w_in, l0_v_norm, l0_w_s, l0_b_s, l0_w_out)
    x = _rglru_layer(x.reshape(BATCH, SEQ, D_MODEL), l1_norm, l1_w_in, l1_conv_w, l1_conv_b, l1_w_a, l1_b_a,
                     l1_w_x, l1_b_x, l1_lam, l1_w_out)
    y = _fox_mixer(x, l2_norm, l2_w_in, l2_b_f, l2_q_norm, l2_k_norm)
    x = _gmlp_layer(x.reshape(n, D_MODEL), l3_norm, l3_w_in, l3_v_norm, l3_w_s, l3_b_s, l3_w_out,
                    pending=(y.reshape(n, E_C), l2_w_out))
    return x.reshape(BATCH, SEQ, D_MODEL)
```

```python
import functools

import jax
import jax.numpy as jnp
from jax import lax
from jax.experimental import pallas as pl
from jax.experimental.pallas import tpu as pltpu

D_MODEL = 1024
BATCH = 8
SEQ = 2048
EPS = 1e-6
CHUNK = 128
E_A = 2 * D_MODEL
G_A = 8
DG_A = E_A // G_A
E_B = 3 * D_MODEL // 2
H_B = 16
BD_B = E_B // H_B
CONV_W = 4
LRU_C = 8.0
H_C = 16
DH_C = D_MODEL // H_C
E_C = H_C * DH_C

LANES = 128
VMEM_LIMIT = 56 * 1024 * 1024

TM_A = 512
WCH_A = 1024
T_B = 64
HG_B = 4
GW_B = HG_B * BD_B
TM_C = 1024
TQ_C = 256
TK_C = 256
NH_C = 4
DEN_ROWS = 16
NEG_BIG = -1e30
LOG2E = 1.4426950408889634

F32 = jnp.float32
BF16 = jnp.bfloat16


def _dot(a, b):
    return jnp.dot(a, b, preferred_element_type=F32)


def _rms(x, g):
    ms = jnp.mean(x * x, axis=-1, keepdims=True)
    return x * lax.rsqrt(ms + EPS) * g


GELU_C = 0.7978845608028654


def _gelu_of_half(hx):
    return hx * (1.0 + jnp.tanh(hx * (2.0 * GELU_C + (8.0 * GELU_C * 0.044715) * (hx * hx))))


def _silu_of_half(hx):
    return hx * (1.0 + jnp.tanh(hx))


def _softplus(x):
    return jnp.maximum(x, 0.0) + jnp.log1p(jnp.exp(-jnp.abs(x)))


def _const_spec(shape):
    n = len(shape)
    return pl.BlockSpec(shape, lambda *_: (0,) * n, pipeline_mode=pl.Buffered(1))


def _win_chunk_copy(win_hbm, stage_scr, sem, c):
    cols = pl.ds(c * WCH_A, WCH_A)
    return pltpu.make_async_copy(win_hbm.at[:, cols], stage_scr.at[c % 2], sem.at[c % 2])


def _gmlp_kernel(has_pending, x_ref, *refs):
    if has_pending:
        yp_ref, wp_ref, *refs = refs
    nrm_ref, win_hbm, vnrm_ref, ws_ref, bst_ref, wo_ref, o_ref, vn_scr, y_scr, win_ref, stage_scr, sem = refs

    @pl.when(pl.program_id(0) == 0)
    def _():
        n_chunks = 3 * E_A // WCH_A
        _win_chunk_copy(win_hbm, stage_scr, sem, 0).start()
        for c in range(n_chunks):
            if c + 1 < n_chunks:
                _win_chunk_copy(win_hbm, stage_scr, sem, c + 1).start()
            _win_chunk_copy(win_hbm, stage_scr, sem, c).wait()
            win_ref[:, c * WCH_A:(c + 1) * WCH_A] = (0.5 * stage_scr[c % 2]).astype(BF16)

    if has_pending:
        x = x_ref[...] + _dot(yp_ref[...], wp_ref[...])
    else:
        x = x_ref[...]
    h = _rms(x, nrm_ref[...]).astype(BF16)
    v = _gelu_of_half(_dot(h, win_ref[:, E_A:2 * E_A]))
    vn_scr[...] = _rms(v, vnrm_ref[...]).astype(BF16)
    row = lax.broadcasted_iota(jnp.int32, (CHUNK, CHUNK), 0)
    col = lax.broadcasted_iota(jnp.int32, (CHUNK, CHUNK), 1)
    causal = col <= row
    for g in range(G_A):
        c0 = g * DG_A
        u = _gelu_of_half(_dot(h, win_ref[:, c0:c0 + DG_A]))
        gate = _silu_of_half(_dot(h, win_ref[:, 2 * E_A + c0:2 * E_A + c0 + DG_A]))
        w = jnp.where(causal, ws_ref[g], 0.0).astype(BF16)
        bias = bst_ref[:, g:g + 1]
        for c in range(TM_A // CHUNK):
            r0 = c * CHUNK
            mixed = _dot(w, vn_scr[r0:r0 + CHUNK, c0:c0 + DG_A]) + bias
            y = u[r0:r0 + CHUNK] * mixed * gate[r0:r0 + CHUNK]
            y_scr[r0:r0 + CHUNK, c0:c0 + DG_A] = y.astype(BF16)
    o_ref[...] = x + _dot(y_scr[...], wo_ref[...])


def _gmlp_layer(x2, norm, w_in, v_norm, w_s, b_s, w_out, pending=None):
    n = x2.shape[0]
    rows = pl.BlockSpec((TM_A, D_MODEL), lambda i: (i, 0))
    pending_specs, pending_args = [], []
    if pending is not None:
        y_prev, w_prev = pending
        pending_specs = [pl.BlockSpec((TM_A, y_prev.shape[1]), lambda i: (i, 0)), _const_spec(w_prev.shape)]
        pending_args = [y_prev, w_prev.astype(BF16)]
    return pl.pallas_call(
        functools.partial(_gmlp_kernel, pending is not None),
        grid=(n // TM_A,),
        in_specs=[
            rows,
            *pending_specs,
            _const_spec((1, D_MODEL)),
            pl.BlockSpec(memory_space=pl.ANY),
            _const_spec((1, E_A)),
            _const_spec((G_A, CHUNK, CHUNK)),
            _const_spec((CHUNK, G_A)),
            _const_spec((E_A, D_MODEL)),
        ],
        out_specs=rows,
        out_shape=jax.ShapeDtypeStruct((n, D_MODEL), F32),
        scratch_shapes=[pltpu.VMEM((TM_A, E_A), BF16), pltpu.VMEM((TM_A, E_A), BF16),
                        pltpu.VMEM((D_MODEL, 3 * E_A), BF16), pltpu.VMEM((2, D_MODEL, WCH_A), F32),
                        pltpu.SemaphoreType.DMA((2,))],
        compiler_params=pltpu.CompilerParams(dimension_semantics=("arbitrary",), vmem_limit_bytes=VMEM_LIMIT),
        name="gmlp_layer",
    )(x2, *pending_args, norm.reshape(1, D_MODEL), w_in, v_norm.reshape(1, E_A), w_s, b_s.T, w_out.astype(BF16))


ROWS_B = T_B * BATCH
TAIL_B = (CONV_W - 1) * BATCH


def _rglru_kernel(x_ref, nrm_ref, win_ref, cw_ref, cb_ref, wbd_ref, hba_ref, hbx_ref, lam_ref, wo_ref, o_ref,
                  hn_scr, xb_scr, a_scr, hs_scr, state_scr, out_scr):
    @pl.when(pl.program_id(0) == 0)
    def _():
        xb_scr[0:TAIL_B, :] = jnp.zeros((TAIL_B, E_B), F32)
        state_scr[...] = jnp.zeros((BATCH, E_B), F32)

    nrm = nrm_ref[...]
    for b in range(BATCH):
        hb = _rms(x_ref[b], nrm)
        for c in range(D_MODEL // LANES):
            hn_scr[c, pl.ds(b, T_B, stride=BATCH), :] = hb[:, c * LANES:(c + 1) * LANES]
    h = jnp.concatenate([hn_scr[c] for c in range(D_MODEL // LANES)], axis=1).astype(BF16)
    xb_scr[TAIL_B:TAIL_B + ROWS_B, :] = _dot(h, win_ref[:, 0:E_B])
    gate = _silu_of_half(_dot(h, win_ref[:, E_B:2 * E_B]))

    xc = cb_ref[...] + cw_ref[0:1, :] * xb_scr[0:ROWS_B, :]
    for k in range(1, CONV_W):
        xc = xc + cw_ref[k:k + 1, :] * xb_scr[k * BATCH:k * BATCH + ROWS_B, :]
    tail = xb_scr[ROWS_B:ROWS_B + TAIL_B, :]
    xb_scr[0:TAIL_B, :] = tail
    xcb = xc.astype(BF16)

    quarter_c_sp = (0.25 * LRU_C) * _softplus(-lam_ref[...])
    for j in range(E_B // GW_B):
        c0 = j * GW_B
        pre = _dot(xcb[:, c0:c0 + GW_B], wbd_ref[j])
        ur = jnp.tanh(pre[:, 0:GW_B] + hba_ref[:, c0:c0 + GW_B])
        ui = jnp.tanh(pre[:, GW_B:2 * GW_B] + hbx_ref[:, c0:c0 + GW_B])
        tn = jnp.tanh(quarter_c_sp[:, c0:c0 + GW_B] * (1.0 + ur))
        d = 1.0 / (1.0 + tn)
        a_scr[:, c0:c0 + GW_B] = (1.0 - tn) * d
        root = jnp.where(tn <= 0.0, 0.0, tn * lax.rsqrt(tn))
        hs_scr[:, c0:c0 + GW_B] = (d * root) * ((1.0 + ui) * xc[:, c0:c0 + GW_B])

    hcur = state_scr[...]
    for t in range(T_B):
        rows = slice(t * BATCH, (t + 1) * BATCH)
        hcur = a_scr[rows, :] * hcur + hs_scr[rows, :]
        hs_scr[rows, :] = hcur
    state_scr[...] = hcur

    y = (hs_scr[...] * gate).astype(BF16)
    out = _dot(y, wo_ref[...])
    for c in range(D_MODEL // LANES):
        out_scr[c] = out[:, c * LANES:(c + 1) * LANES]
    for b in range(BATCH):
        ob = jnp.concatenate([out_scr[c, pl.ds(b, T_B, stride=BATCH), :] for c in range(D_MODEL // LANES)], axis=1)
        o_ref[b] = x_ref[b] + ob


def _block_diag_gates(w_a, w_x):
    def bd(w):
        wg = w.reshape(H_B // HG_B, HG_B, BD_B, BD_B)
        eye = jnp.eye(HG_B, dtype=w.dtype)
        return jnp.einsum('ghij,hk->ghikj', wg, eye).reshape(H_B // HG_B, GW_B, GW_B)
    return jnp.concatenate([bd(w_a), bd(w_x)], axis=-1)


def _rglru_layer(x3, norm, w_in, conv_w, conv_b, w_a, b_a, w_x, b_x, lam, w_out):
    ng = H_B // HG_B
    w_half = jnp.concatenate([w_in[:, 0:E_B], 0.5 * w_in[:, E_B:2 * E_B]], axis=1)
    return pl.pallas_call(
        _rglru_kernel,
        grid=(SEQ // T_B,),
        in_specs=[
            pl.BlockSpec((BATCH, T_B, D_MODEL), lambda i: (0, i, 0)),
            _const_spec((1, D_MODEL)),
            _const_spec((D_MODEL, 2 * E_B)),
            _const_spec((CONV_W, E_B)),
            _const_spec((1, E_B)),
            _const_spec((ng, GW_B, 2 * GW_B)),
            _const_spec((1, E_B)),
            _const_spec((1, E_B)),
            _const_spec((1, E_B)),
            _const_spec((E_B, D_MODEL)),
        ],
        out_specs=pl.BlockSpec((BATCH, T_B, D_MODEL), lambda i: (0, i, 0)),
        out_shape=jax.ShapeDtypeStruct((BATCH, SEQ, D_MODEL), F32),
        scratch_shapes=[
            pltpu.VMEM((D_MODEL // LANES, ROWS_B, LANES), F32),
            pltpu.VMEM((TAIL_B + ROWS_B, E_B), F32),
            pltpu.VMEM((ROWS_B, E_B), F32),
            pltpu.VMEM((ROWS_B, E_B), F32),
            pltpu.VMEM((BATCH, E_B), F32),
            pltpu.VMEM((D_MODEL // LANES, ROWS_B, LANES), F32),
        ],
        compiler_params=pltpu.CompilerParams(dimension_semantics=("arbitrary",), vmem_limit_bytes=VMEM_LIMIT),
        name="rglru_layer",
    )(x3, norm.reshape(1, D_MODEL), w_half.astype(BF16), conv_w, conv_b.reshape(1, E_B),
      (0.5 * _block_diag_gates(w_a, w_x)).astype(BF16), (0.5 * b_a).reshape(1, E_B), (0.5 * b_x).reshape(1, E_B),
      lam.reshape(1, E_B), w_out.astype(BF16))


N_SPLIT = 3


def _fox_proj_kernel(x_ref, nrm_ref, wkg_ref, wqvt_ref, wf_ref, bf_ref, kg_ref, seg_ref, segt_ref, place_ref, tri_ref,
                     qt_ref, k_ref, ek_ref, vt_ref, sg_ref, cum_ref, carry_scr):
    @pl.when(pl.program_id(1) == 0)
    def _():
        carry_scr[...] = jnp.zeros((1, LANES), F32)

    h = _rms(x_ref[0], nrm_ref[...]).astype(BF16)

    zk = _dot(h, wkg_ref[:, 0:E_C])
    ss = _dot((zk * zk).astype(BF16), seg_ref[...])
    r = lax.rsqrt(ss * (1.0 / DH_C) + EPS)
    r_hi = r.astype(BF16)
    r_lo = (r - r_hi.astype(F32)).astype(BF16)
    rexp = _dot(jnp.concatenate([r_hi, r_lo], axis=1), segt_ref[...])
    k_ref[0] = (zk * rexp * kg_ref[...]).astype(BF16)
    sg_ref[0] = _silu_of_half(_dot(h, wkg_ref[:, E_C:2 * E_C])).astype(BF16)

    zqv = lax.dot_general(wqvt_ref[...], h, (((1,), (1,)), ((), ())), preferred_element_type=F32)
    vt_ref[0] = zqv[E_C:2 * E_C].astype(BF16)
    for hd in range(H_C):
        rows = slice(hd * DH_C, (hd + 1) * DH_C)
        zq = zqv[rows]
        rq = lax.rsqrt(jnp.mean(zq * zq, axis=0, keepdims=True) + EPS)
        qt_ref[0, rows, :] = (zq * rq).astype(BF16)

    z = _dot(h, wf_ref[...]) + bf_ref[...]
    log_f = jnp.minimum(z, 0.0) - jnp.log1p(jnp.exp(-jnp.abs(z)))
    group = lax.broadcasted_iota(jnp.int32, (TM_C, LANES), 1) // H_C

    def split3(v):
        hi = v.astype(BF16).astype(F32)
        mid = (v - hi).astype(BF16).astype(F32)
        lo = v - hi - mid
        return jnp.where(group == 0, hi, jnp.where(group == 1, mid, lo)).astype(BF16)

    cp = _dot(tri_ref[...], split3(log_f))
    cum = cp + pltpu.roll(cp, LANES - H_C, axis=1) + pltpu.roll(cp, LANES - 2 * H_C, axis=1) + carry_scr[...]
    carry_scr[...] = cum[TM_C - 1:TM_C, :]
    c2 = cum * LOG2E
    cum_ref[0] = c2[:, 0:H_C]
    c2 = jnp.where(group == 0, c2,
                   jnp.where(group == 1, pltpu.roll(c2, H_C, axis=1), pltpu.roll(c2, 2 * H_C, axis=1)))
    ek_ref[0] = _dot(split3(c2), place_ref[...]).astype(BF16)


def _fox_proj(x3, norm, w_in, b_f, q_norm, k_norm):
    w_kg = jnp.concatenate([w_in[:, E_C:2 * E_C], 0.5 * w_in[:, 3 * E_C:4 * E_C]], axis=1).astype(BF16)
    w_qvt = jnp.concatenate([w_in[:, 0:E_C].astype(BF16).T, w_in[:, 2 * E_C:3 * E_C].astype(BF16).T], axis=0)
    pad = LANES - N_SPLIT * H_C
    w_f = jnp.pad(jnp.tile(w_in[:, 4 * E_C:], (1, N_SPLIT)), ((0, 0), (0, pad))).astype(BF16)
    b_fp = jnp.pad(jnp.tile(b_f, N_SPLIT), (0, pad)).reshape(1, LANES)
    kg = jnp.tile(k_norm * q_norm, H_C).reshape(1, E_C) * (DH_C ** -0.5 * LOG2E)
    head_of = jnp.arange(E_C) // DH_C
    seg = (head_of[:, None] == jnp.arange(LANES)[None, :]).astype(BF16)
    segt2 = jnp.concatenate([seg.T, seg.T], axis=0)
    src = jnp.arange(LANES)
    dst = (src % H_C) * DH_C + src // H_C
    place = jnp.where((src[:, None] < N_SPLIT * H_C) & (dst[:, None] == jnp.arange(E_C)[None, :]), -1.0, 0.0)
    tok = jax.ShapeDtypeStruct((BATCH, SEQ, E_C), BF16)
    tok_t = jax.ShapeDtypeStruct((BATCH, E_C, SEQ), BF16)
    blk = pl.BlockSpec((1, TM_C, E_C), lambda b, i: (b, i, 0))
    blk_t = pl.BlockSpec((1, E_C, TM_C), lambda b, i: (b, 0, i))
    return pl.pallas_call(
        _fox_proj_kernel,
        grid=(BATCH, SEQ // TM_C),
        in_specs=[
            pl.BlockSpec((1, TM_C, D_MODEL), lambda b, i: (b, i, 0)),
            _const_spec((1, D_MODEL)),
            _const_spec((D_MODEL, 2 * E_C)),
            _const_spec((2 * E_C, D_MODEL)),
            _const_spec((D_MODEL, LANES)),
            _const_spec((1, LANES)),
            _const_spec((1, E_C)),
            _const_spec((E_C, LANES)),
            _const_spec((2 * LANES, E_C)),
            _const_spec((LANES, E_C)),
            _const_spec((TM_C, TM_C)),
        ],
        out_specs=[blk_t, blk, blk, blk_t, blk, pl.BlockSpec((1, TM_C, H_C), lambda b, i: (b, i, 0))],
        out_shape=[tok_t, tok, tok, tok_t, tok, jax.ShapeDtypeStruct((BATCH, SEQ, H_C), F32)],
        scratch_shapes=[pltpu.VMEM((1, LANES), F32)],
        compiler_params=pltpu.CompilerParams(dimension_semantics=("arbitrary", "arbitrary"),
                                             vmem_limit_bytes=VMEM_LIMIT),
        name="fox_proj",
    )(x3, norm.reshape(1, D_MODEL), w_kg, w_qvt, w_f, b_fp, kg, seg, segt2, place.astype(BF16),
      jnp.tril(jnp.ones((TM_C, TM_C), BF16)))


def _attn_kernel(qt_ref, k_ref, ek_ref, vt_ref, sg_ref, cq_ref, o_ref,
                 qa_scr, acc_scr, m_scr, sta_scr, stb_scr, mxa_scr, mxb_scr):
    causal = (lax.broadcasted_iota(jnp.int32, (TK_C, TQ_C), 0) <= lax.broadcasted_iota(jnp.int32, (TK_C, TQ_C), 1))
    ones_v = jnp.ones((DEN_ROWS, TK_C), BF16)
    zero_q = jnp.zeros((DH_C, TQ_C), BF16)
    row_q = lax.broadcasted_iota(jnp.int32, (LANES, TQ_C), 0)
    heads = range(NH_C)

    for hh in heads:
        odd = hh % 2
        qa_scr[hh, (1 - odd) * DH_C:(2 - odd) * DH_C, :] = zero_q
        mine = jnp.logical_and(row_q >= odd * DH_C, row_q < odd * DH_C + N_SPLIT)
        qa_scr[hh, LANES:2 * LANES, :] = jnp.where(mine, 1.0, 0.0).astype(BF16)

    def load_queries(qi):
        cols = slice(qi * TQ_C, (qi + 1) * TQ_C)
        for hh in heads:
            odd = hh % 2
            qa_scr[hh, odd * DH_C:(odd + 1) * DH_C, :] = qt_ref[0, hh * DH_C:(hh + 1) * DH_C, cols]

    def scores(qi, j, st_ref, mx_ref):
        keys = slice(j * TK_C, (j + 1) * TK_C)
        for hh in heads:
            pair = hh // 2
            ka = jnp.concatenate([k_ref[0, keys, pair * LANES:(pair + 1) * LANES],
                                  ek_ref[0, keys, pair * LANES:(pair + 1) * LANES]], axis=1)
            st = _dot(ka, qa_scr[hh])
            if j == qi:
                st = jnp.where(causal, st, NEG_BIG)
            st_ref[hh] = st
            mx_ref[hh] = jnp.max(st, axis=0, keepdims=True)

    def consume(qi, j, st_ref, mx_ref):
        alpha, pt = [], []
        for hh in heads:
            cq = cq_ref[0, 0, hh:hh + 1, qi * TQ_C:(qi + 1) * TQ_C]
            if j == 0:
                m = mx_ref[hh] + cq
            else:
                m_old = m_scr[hh]
                m = jnp.maximum(m_old, mx_ref[hh] + cq)
                alpha.append(jnp.exp2(m_old - m))
            pt.append(jnp.exp2((st_ref[hh] - (m - cq)).astype(BF16)))
            m_scr[hh] = m
        pv = []
        for hh in heads:
            vt = vt_ref[0, hh * DH_C:(hh + 1) * DH_C, j * TK_C:(j + 1) * TK_C]
            pv.append(_dot(jnp.concatenate([vt, ones_v], axis=0), pt[hh]))
        for hh in heads:
            acc_scr[hh] = pv[hh] if j == 0 else alpha[hh] * acc_scr[hh] + pv[hh]

    def finish(qi):
        rows = slice(qi * TQ_C, (qi + 1) * TQ_C)
        for pair in range(NH_C // 2):
            a0 = acc_scr[2 * pair]
            a1 = acc_scr[2 * pair + 1]
            ot = jnp.concatenate([a0[0:DH_C] / a0[DH_C:DH_C + 1], a1[0:DH_C] / a1[DH_C:DH_C + 1]], axis=0)
            cols = slice(pair * LANES, (pair + 1) * LANES)
            o_ref[0, rows, cols] = (ot.T * sg_ref[0, rows, cols].astype(F32)).astype(BF16)

    tiles = [(qi, j) for qi in range(SEQ // TQ_C) for j in range(qi + 1)]
    bufs = ((sta_scr, mxa_scr), (stb_scr, mxb_scr))
    def issue(n):
        qi, j = tiles[n]
        if j == 0:
            load_queries(qi)
        scores(qi, j, *bufs[n % 2])

    issue(0)
    issue(1)
    for n, (qi, j) in enumerate(tiles):
        consume(qi, j, *bufs[n % 2])
        if j == qi:
            finish(qi)
        if n + 2 < len(tiles):
            issue(n + 2)


def _fox_attention(qt, k, ek, vt, sg, cum):
    assert TQ_C == TK_C
    ng = H_C // NH_C
    w = NH_C * DH_C
    cq = cum.transpose(0, 2, 1).reshape(BATCH, ng, NH_C, SEQ)
    blk = pl.BlockSpec((1, SEQ, w), lambda b, g: (b, 0, g))
    blk_t = pl.BlockSpec((1, w, SEQ), lambda b, g: (b, g, 0))
    return pl.pallas_call(
        _attn_kernel,
        grid=(BATCH, ng),
        in_specs=[
            blk_t, blk, blk, blk_t, blk,
            pl.BlockSpec((1, 1, NH_C, SEQ), lambda b, g: (b, g, 0, 0)),
        ],
        out_specs=blk,
        out_shape=jax.ShapeDtypeStruct((BATCH, SEQ, E_C), BF16),
        scratch_shapes=[pltpu.VMEM((NH_C, 2 * LANES, TQ_C), BF16), pltpu.VMEM((NH_C, DH_C + DEN_ROWS, TQ_C), F32),
                        pltpu.VMEM((NH_C, 1, TQ_C), F32),
                        pltpu.VMEM((NH_C, TK_C, TQ_C), F32), pltpu.VMEM((NH_C, TK_C, TQ_C), F32),
                        pltpu.VMEM((NH_C, 1, TQ_C), F32), pltpu.VMEM((NH_C, 1, TQ_C), F32)],
        compiler_params=pltpu.CompilerParams(dimension_semantics=("arbitrary", "arbitrary"),
                                             vmem_limit_bytes=VMEM_LIMIT),
        name="fox_attention",
    )(qt, k, ek, vt, sg, cq)


def _fox_mixer(x3, norm, w_in, b_f, q_norm, k_norm):
    qt, k, ek, vt, sg, cum = _fox_proj(x3, norm, w_in, b_f, q_norm, k_norm)
    return _fox_attention(qt, k, ek, vt, sg, cum)


def kernel(x, l0_norm, l0_w_in, l0_v_norm, l0_w_s, l0_b_s, l0_w_out, l1_norm, l1_w_in, l1_conv_w, l1_conv_b, l1_w_a, l1_b_a, l1_w_x, l1_b_x, l1_lam, l1_w_out, l2_norm, l2_w_in, l2_b_f, l2_q_norm, l2_k_norm, l2_w_out, l3_norm, l3_w_in, l3_v_norm, l3_w_s, l3_b_s, l3_w_out):
    n = BATCH * SEQ
    x = _gmlp_layer(x.reshape(n, D_MODEL), l0_norm, l0_w_in, l0_v_norm, l0_w_s, l0_b_s, l0_w_out)
    x = _rglru_layer(x.reshape(BATCH, SEQ, D_MODEL), l1_norm, l1_w_in, l1_conv_w, l1_conv_b, l1_w_a, l1_b_a,
                     l1_w_x, l1_b_x, l1_lam, l1_w_out)
    y = _fox_mixer(x, l2_norm, l2_w_in, l2_b_f, l2_q_norm, l2_k_norm)
    x = _gmlp_layer(x.reshape(n, D_MODEL), l3_norm, l3_w_in, l3_v_norm, l3_w_s, l3_b_s, l3_w_out,
                    pending=(y.reshape(n, E_C), l2_w_out))
    return x.reshape(BATCH, SEQ, D_MODEL)
```

```python
import functools

import jax
import jax.numpy as jnp
from jax import lax
from jax.experimental import pallas as pl
from jax.experimental.pallas import tpu as pltpu

D_MODEL = 1024
BATCH = 8
SEQ = 2048
EPS = 1e-6
CHUNK = 128
E_A = 2 * D_MODEL
G_A = 8
DG_A = E_A // G_A
E_B = 3 * D_MODEL // 2
H_B = 16
BD_B = E_B // H_B
CONV_W = 4
LRU_C = 8.0
H_C = 16
DH_C = D_MODEL // H_C
E_C = H_C * DH_C

LANES = 128
VMEM_LIMIT = 56 * 1024 * 1024

TM_A = 512
WCH_A = 1024
T_B = 64
HG_B = 4
GW_B = HG_B * BD_B
TM_C = 1024
TQ_C = 256
TK_C = 256
NH_C = 4
DEN_ROWS = 16
NEG_BIG = -1e30
LOG2E = 1.4426950408889634

F32 = jnp.float32
BF16 = jnp.bfloat16


def _dot(a, b):
    return jnp.dot(a, b, preferred_element_type=F32)


def _rms(x, g):
    ms = jnp.mean(x * x, axis=-1, keepdims=True)
    return x * lax.rsqrt(ms + EPS) * g


GELU_C = 0.7978845608028654


def _gelu_of_half(hx):
    return hx * (1.0 + jnp.tanh(hx * (2.0 * GELU_C + (8.0 * GELU_C * 0.044715) * (hx * hx))))


def _silu_of_half(hx):
    return hx * (1.0 + jnp.tanh(hx))


def _softplus(x):
    return jnp.maximum(x, 0.0) + jnp.log1p(jnp.exp(-jnp.abs(x)))


def _const_spec(shape):
    n = len(shape)
    return pl.BlockSpec(shape, lambda *_: (0,) * n, pipeline_mode=pl.Buffered(1))


def _win_chunk_copy(win_hbm, stage_scr, sem, c):
    cols = pl.ds(c * WCH_A, WCH_A)
    return pltpu.make_async_copy(win_hbm.at[:, cols], stage_scr.at[c % 2], sem.at[c % 2])


def _gmlp_kernel(has_pending, x_ref, *refs):
    if has_pending:
        yp_ref, wp_ref, *refs = refs
    nrm_ref, win_hbm, vnrm_ref, ws_ref, bst_ref, wo_ref, o_ref, vn_scr, y_scr, win_ref, stage_scr, sem = refs

    @pl.when(pl.program_id(0) == 0)
    def _():
        n_chunks = 3 * E_A // WCH_A
        _win_chunk_copy(win_hbm, stage_scr, sem, 0).start()
        for c in range(n_chunks):
            if c + 1 < n_chunks:
                _win_chunk_copy(win_hbm, stage_scr, sem, c + 1).start()
            _win_chunk_copy(win_hbm, stage_scr, sem, c).wait()
            win_ref[:, c * WCH_A:(c + 1) * WCH_A] = (0.5 * stage_scr[c % 2]).astype(BF16)

    if has_pending:
        x = x_ref[...] + _dot(yp_ref[...], wp_ref[...])
    else:
        x = x_ref[...]
    h = _rms(x, nrm_ref[...]).astype(BF16)
    v = _gelu_of_half(_dot(h, win_ref[:, E_A:2 * E_A]))
    vn_scr[...] = _rms(v, vnrm_ref[...]).astype(BF16)
    row = lax.broadcasted_iota(jnp.int32, (CHUNK, CHUNK), 0)
    col = lax.broadcasted_iota(jnp.int32, (CHUNK, CHUNK), 1)
    causal = col <= row
    for g in range(G_A):
        c0 = g * DG_A
        u = _gelu_of_half(_dot(h, win_ref[:, c0:c0 + DG_A]))
        gate = _silu_of_half(_dot(h, win_ref[:, 2 * E_A + c0:2 * E_A + c0 + DG_A]))
        w = jnp.where(causal, ws_ref[g], 0.0).astype(BF16)
        bias = bst_ref[:, g:g + 1]
        for c in range(TM_A // CHUNK):
            r0 = c * CHUNK
            mixed = _dot(w, vn_scr[r0:r0 + CHUNK, c0:c0 + DG_A]) + bias
            y = u[r0:r0 + CHUNK] * mixed * gate[r0:r0 + CHUNK]
            y_scr[r0:r0 + CHUNK, c0:c0 + DG_A] = y.astype(BF16)
    o_ref[...] = x + _dot(y_scr[...], wo_ref[...])


def _gmlp_layer(x2, norm, w_in, v_norm, w_s, b_s, w_out, pending=None):
    n = x2.shape[0]
    rows = pl.BlockSpec((TM_A, D_MODEL), lambda i: (i, 0))
    pending_specs, pending_args = [], []
    if pending is not None:
        y_prev, w_prev = pending
        pending_specs = [pl.BlockSpec((TM_A, y_prev.shape[1]), lambda i: (i, 0)), _const_spec(w_prev.shape)]
        pending_args = [y_prev, w_prev.astype(BF16)]
    return pl.pallas_call(
        functools.partial(_gmlp_kernel, pending is not None),
        grid=(n // TM_A,),
        in_specs=[
            rows,
            *pending_specs,
            _const_spec((1, D_MODEL)),
            pl.BlockSpec(memory_space=pl.ANY),
            _const_spec((1, E_A)),
            _const_spec((G_A, CHUNK, CHUNK)),
            _const_spec((CHUNK, G_A)),
            _const_spec((E_A, D_MODEL)),
        ],
        out_specs=rows,
        out_shape=jax.ShapeDtypeStruct((n, D_MODEL), F32),
        scratch_shapes=[pltpu.VMEM((TM_A, E_A), BF16), pltpu.VMEM((TM_A, E_A), BF16),
                        pltpu.VMEM((D_MODEL, 3 * E_A), BF16), pltpu.VMEM((2, D_MODEL, WCH_A), F32),
                        pltpu.SemaphoreType.DMA((2,))],
        compiler_params=pltpu.CompilerParams(dimension_semantics=("arbitrary",), vmem_limit_bytes=VMEM_LIMIT),
        name="gmlp_layer",
    )(x2, *pending_args, norm.reshape(1, D_MODEL), w_in, v_norm.reshape(1, E_A), w_s, b_s.T, w_out.astype(BF16))


ROWS_B = T_B * BATCH
TAIL_B = (CONV_W - 1) * BATCH


def _rglru_kernel(x_ref, nrm_ref, win_ref, cw_ref, cb_ref, wbd_ref, hba_ref, hbx_ref, lam_ref, wo_ref, o_ref,
                  hn_scr, xb_scr, a_scr, hs_scr, state_scr, out_scr):
    @pl.when(pl.program_id(0) == 0)
    def _():
        xb_scr[0:TAIL_B, :] = jnp.zeros((TAIL_B, E_B), F32)
        state_scr[...] = jnp.zeros((BATCH, E_B), F32)

    nrm = nrm_ref[...]
    for b in range(BATCH):
        hb = _rms(x_ref[b], nrm)
        for c in range(D_MODEL // LANES):
            hn_scr[c, pl.ds(b, T_B, stride=BATCH), :] = hb[:, c * LANES:(c + 1) * LANES]
    h = jnp.concatenate([hn_scr[c] for c in range(D_MODEL // LANES)], axis=1).astype(BF16)
    xb_scr[TAIL_B:TAIL_B + ROWS_B, :] = _dot(h, win_ref[:, 0:E_B])
    gate = _silu_of_half(_dot(h, win_ref[:, E_B:2 * E_B]))

    xc = cb_ref[...] + cw_ref[0:1, :] * xb_scr[0:ROWS_B, :]
    for k in range(1, CONV_W):
        xc = xc + cw_ref[k:k + 1, :] * xb_scr[k * BATCH:k * BATCH + ROWS_B, :]
    tail = xb_scr[ROWS_B:ROWS_B + TAIL_B, :]
    xb_scr[0:TAIL_B, :] = tail
    xcb = xc.astype(BF16)

    quarter_c_sp = (0.25 * LRU_C) * _softplus(-lam_ref[...])
    for j in range(E_B // GW_B):
        c0 = j * GW_B
        pre = _dot(xcb[:, c0:c0 + GW_B], wbd_ref[j])
        ur = jnp.tanh(pre[:, 0:GW_B] + hba_ref[:, c0:c0 + GW_B])
        ui = jnp.tanh(pre[:, GW_B:2 * GW_B] + hbx_ref[:, c0:c0 + GW_B])
        tn = jnp.tanh(quarter_c_sp[:, c0:c0 + GW_B] * (1.0 + ur))
        d = 1.0 / (1.0 + tn)
        a_scr[:, c0:c0 + GW_B] = (1.0 - tn) * d
        root = jnp.where(tn <= 0.0, 0.0, tn * lax.rsqrt(tn))
        hs_scr[:, c0:c0 + GW_B] = (d * root) * ((1.0 + ui) * xc[:, c0:c0 + GW_B])

    hcur = state_scr[...]
    for t in range(T_B):
        rows = slice(t * BATCH, (t + 1) * BATCH)
        hcur = a_scr[rows, :] * hcur + hs_scr[rows, :]
        hs_scr[rows, :] = hcur
    state_scr[...] = hcur

    y = (hs_scr[...] * gate).astype(BF16)
    out = _dot(y, wo_ref[...])
    for c in range(D_MODEL // LANES):
        out_scr[c] = out[:, c * LANES:(c + 1) * LANES]
    for b in range(BATCH):
        ob = jnp.concatenate([out_scr[c, pl.ds(b, T_B, stride=BATCH), :] for c in range(D_MODEL // LANES)], axis=1)
        o_ref[b] = x_ref[b] + ob


def _block_diag_gates(w_a, w_x):
    def bd(w):
        wg = w.reshape(H_B // HG_B, HG_B, BD_B, BD_B)
        eye = jnp.eye(HG_B, dtype=w.dtype)
        return jnp.einsum('ghij,hk->ghikj', wg, eye).reshape(H_B // HG_B, GW_B, GW_B)
    return jnp.concatenate([bd(w_a), bd(w_x)], axis=-1)


def _rglru_layer(x3, norm, w_in, conv_w, conv_b, w_a, b_a, w_x, b_x, lam, w_out):
    ng = H_B // HG_B
    w_half = jnp.concatenate([w_in[:, 0:E_B].astype(BF16), (0.5 * w_in[:, E_B:2 * E_B]).astype(BF16)], axis=1)
    return pl.pallas_call(
        _rglru_kernel,
        grid=(SEQ // T_B,),
        in_specs=[
            pl.BlockSpec((BATCH, T_B, D_MODEL), lambda i: (0, i, 0)),
            _const_spec((1, D_MODEL)),
            _const_spec((D_MODEL, 2 * E_B)),
            _const_spec((CONV_W, E_B)),
            _const_spec((1, E_B)),
            _const_spec((ng, GW_B, 2 * GW_B)),
            _const_spec((1, E_B)),
            _const_spec((1, E_B)),
            _const_spec((1, E_B)),
            _const_spec((E_B, D_MODEL)),
        ],
        out_specs=pl.BlockSpec((BATCH, T_B, D_MODEL), lambda i: (0, i, 0)),
        out_shape=jax.ShapeDtypeStruct((BATCH, SEQ, D_MODEL), F32),
        scratch_shapes=[
            pltpu.VMEM((D_MODEL // LANES, ROWS_B, LANES), F32),
            pltpu.VMEM((TAIL_B + ROWS_B, E_B), F32),
            pltpu.VMEM((ROWS_B, E_B), F32),
            pltpu.VMEM((ROWS_B, E_B), F32),
            pltpu.VMEM((BATCH, E_B), F32),
            pltpu.VMEM((D_MODEL // LANES, ROWS_B, LANES), F32),
        ],
        compiler_params=pltpu.CompilerParams(dimension_semantics=("arbitrary",), vmem_limit_bytes=VMEM_LIMIT),
        name="rglru_layer",
    )(x3, norm.reshape(1, D_MODEL), w_half, conv_w, conv_b.reshape(1, E_B),
      (0.5 * _block_diag_gates(w_a, w_x)).astype(BF16), (0.5 * b_a).reshape(1, E_B), (0.5 * b_x).reshape(1, E_B),
      lam.reshape(1, E_B), w_out.astype(BF16))


N_SPLIT = 3


def _fox_proj_kernel(x_ref, nrm_ref, wkg_ref, wqvt_ref, wf_ref, bf_ref, kg_ref, seg_ref, segt_ref, place_ref, tri_ref,
                     qt_ref, k_ref, ek_ref, vt_ref, sg_ref, cum_ref, carry_scr):
    @pl.when(pl.program_id(1) == 0)
    def _():
        carry_scr[...] = jnp.zeros((1, LANES), F32)

    h = _rms(x_ref[0], nrm_ref[...]).astype(BF16)

    zk = _dot(h, wkg_ref[:, 0:E_C])
    ss = _dot((zk * zk).astype(BF16), seg_ref[...])
    r = lax.rsqrt(ss * (1.0 / DH_C) + EPS)
    r_hi = r.astype(BF16)
    r_lo = (r - r_hi.astype(F32)).astype(BF16)
    rexp = _dot(jnp.concatenate([r_hi, r_lo], axis=1), segt_ref[...])
    k_ref[0] = (zk * rexp * kg_ref[...]).astype(BF16)
    sg_ref[0] = _silu_of_half(_dot(h, wkg_ref[:, E_C:2 * E_C])).astype(BF16)

    zqv = lax.dot_general(wqvt_ref[...], h, (((1,), (1,)), ((), ())), preferred_element_type=F32)
    vt_ref[0] = zqv[E_C:2 * E_C].astype(BF16)
    for hd in range(H_C):
        rows = slice(hd * DH_C, (hd + 1) * DH_C)
        zq = zqv[rows]
        rq = lax.rsqrt(jnp.mean(zq * zq, axis=0, keepdims=True) + EPS)
        qt_ref[0, rows, :] = (zq * rq).astype(BF16)

    z = _dot(h, wf_ref[...]) + bf_ref[...]
    log_f = jnp.minimum(z, 0.0) - jnp.log1p(jnp.exp(-jnp.abs(z)))
    group = lax.broadcasted_iota(jnp.int32, (TM_C, LANES), 1) // H_C

    def split3(v):
        hi = v.astype(BF16).astype(F32)
        mid = (v - hi).astype(BF16).astype(F32)
        lo = v - hi - mid
        return jnp.where(group == 0, hi, jnp.where(group == 1, mid, lo)).astype(BF16)

    cp = _dot(tri_ref[...], split3(log_f))
    cum = cp + pltpu.roll(cp, LANES - H_C, axis=1) + pltpu.roll(cp, LANES - 2 * H_C, axis=1) + carry_scr[...]
    carry_scr[...] = cum[TM_C - 1:TM_C, :]
    c2 = cum * LOG2E
    cum_ref[0] = c2[:, 0:H_C]
    c2 = jnp.where(group == 0, c2,
                   jnp.where(group == 1, pltpu.roll(c2, H_C, axis=1), pltpu.roll(c2, 2 * H_C, axis=1)))
    ek_ref[0] = _dot(split3(c2), place_ref[...]).astype(BF16)


def _fox_proj(x3, norm, w_in, b_f, q_norm, k_norm):
    w_kg = jnp.concatenate([w_in[:, E_C:2 * E_C].astype(BF16), (0.5 * w_in[:, 3 * E_C:4 * E_C]).astype(BF16)], axis=1)
    w_qvt = jnp.concatenate([w_in[:, 0:E_C].astype(BF16).T, w_in[:, 2 * E_C:3 * E_C].astype(BF16).T], axis=0)
    pad = LANES - N_SPLIT * H_C
    w_f = jnp.pad(jnp.tile(w_in[:, 4 * E_C:], (1, N_SPLIT)), ((0, 0), (0, pad))).astype(BF16)
    b_fp = jnp.pad(jnp.tile(b_f, N_SPLIT), (0, pad)).reshape(1, LANES)
    kg = jnp.tile(k_norm * q_norm, H_C).reshape(1, E_C) * (DH_C ** -0.5 * LOG2E)
    head_of = jnp.arange(E_C) // DH_C
    seg = (head_of[:, None] == jnp.arange(LANES)[None, :]).astype(BF16)
    segt2 = jnp.concatenate([seg.T, seg.T], axis=0)
    src = jnp.arange(LANES)
    dst = (src % H_C) * DH_C + src // H_C
    place = jnp.where((src[:, None] < N_SPLIT * H_C) & (dst[:, None] == jnp.arange(E_C)[None, :]), -1.0, 0.0)
    tok = jax.ShapeDtypeStruct((BATCH, SEQ, E_C), BF16)
    tok_t = jax.ShapeDtypeStruct((BATCH, E_C, SEQ), BF16)
    blk = pl.BlockSpec((1, TM_C, E_C), lambda b, i: (b, i, 0))
    blk_t = pl.BlockSpec((1, E_C, TM_C), lambda b, i: (b, 0, i))
    return pl.pallas_call(
        _fox_proj_kernel,
        grid=(BATCH, SEQ // TM_C),
        in_specs=[
            pl.BlockSpec((1, TM_C, D_MODEL), lambda b, i: (b, i, 0)),
            _const_spec((1, D_MODEL)),
            _const_spec((D_MODEL, 2 * E_C)),
            _const_spec((2 * E_C, D_MODEL)),
            _const_spec((D_MODEL, LANES)),
            _const_spec((1, LANES)),
            _const_spec((1, E_C)),
            _const_spec((E_C, LANES)),
            _const_spec((2 * LANES, E_C)),
            _const_spec((LANES, E_C)),
            _const_spec((TM_C, TM_C)),
        ],
        out_specs=[blk_t, blk, blk, blk_t, blk, pl.BlockSpec((1, TM_C, H_C), lambda b, i: (b, i, 0))],
        out_shape=[tok_t, tok, tok, tok_t, tok, jax.ShapeDtypeStruct((BATCH, SEQ, H_C), F32)],
        scratch_shapes=[pltpu.VMEM((1, LANES), F32)],
        compiler_params=pltpu.CompilerParams(dimension_semantics=("arbitrary", "arbitrary"),
                                             vmem_limit_bytes=VMEM_LIMIT),
        name="fox_proj",
    )(x3, norm.reshape(1, D_MODEL), w_kg, w_qvt, w_f, b_fp, kg, seg, segt2, place.astype(BF16),
      jnp.tril(jnp.ones((TM_C, TM_C), BF16)))


def _attn_kernel(qt_ref, k_ref, ek_ref, vt_ref, sg_ref, cq_ref, o_ref,
                 qa_scr, acc_scr, m_scr, sta_scr, stb_scr, mxa_scr, mxb_scr):
    causal = (lax.broadcasted_iota(jnp.int32, (TK_C, TQ_C), 0) <= lax.broadcasted_iota(jnp.int32, (TK_C, TQ_C), 1))
    ones_v = jnp.ones((DEN_ROWS, TK_C), BF16)
    zero_q = jnp.zeros((DH_C, TQ_C), BF16)
    row_q = lax.broadcasted_iota(jnp.int32, (LANES, TQ_C), 0)
    heads = range(NH_C)

    for hh in heads:
        odd = hh % 2
        qa_scr[hh, (1 - odd) * DH_C:(2 - odd) * DH_C, :] = zero_q
        mine = jnp.logical_and(row_q >= odd * DH_C, row_q < odd * DH_C + N_SPLIT)
        qa_scr[hh, LANES:2 * LANES, :] = jnp.where(mine, 1.0, 0.0).astype(BF16)

    def load_queries(qi):
        cols = slice(qi * TQ_C, (qi + 1) * TQ_C)
        for hh in heads:
            odd = hh % 2
            qa_scr[hh, odd * DH_C:(odd + 1) * DH_C, :] = qt_ref[0, hh * DH_C:(hh + 1) * DH_C, cols]

    def scores(qi, j, st_ref, mx_ref):
        keys = slice(j * TK_C, (j + 1) * TK_C)
        for hh in heads:
            pair = hh // 2
            ka = jnp.concatenate([k_ref[0, keys, pair * LANES:(pair + 1) * LANES],
                                  ek_ref[0, keys, pair * LANES:(pair + 1) * LANES]], axis=1)
            st = _dot(ka, qa_scr[hh])
            if j == qi:
                st = jnp.where(causal, st, NEG_BIG)
            st_ref[hh] = st
            mx_ref[hh] = jnp.max(st, axis=0, keepdims=True)

    def consume(qi, j, st_ref, mx_ref):
        alpha, pt = [], []
        for hh in heads:
            cq = cq_ref[0, 0, hh:hh + 1, qi * TQ_C:(qi + 1) * TQ_C]
            if j == 0:
                m = mx_ref[hh] + cq
            else:
                m_old = m_scr[hh]
                m = jnp.maximum(m_old, mx_ref[hh] + cq)
                alpha.append(jnp.exp2(m_old - m))
            pt.append(jnp.exp2((st_ref[hh] - (m - cq)).astype(BF16)))
            m_scr[hh] = m
        pv = []
        for hh in heads:
            vt = vt_ref[0, hh * DH_C:(hh + 1) * DH_C, j * TK_C:(j + 1) * TK_C]
            pv.append(_dot(jnp.concatenate([vt, ones_v], axis=0), pt[hh]))
        for hh in heads:
            acc_scr[hh] = pv[hh] if j == 0 else alpha[hh] * acc_scr[hh] + pv[hh]

    def finish(qi):
        rows = slice(qi * TQ_C, (qi + 1) * TQ_C)
        for pair in range(NH_C // 2):
            a0 = acc_scr[2 * pair]
            a1 = acc_scr[2 * pair + 1]
            ot = jnp.concatenate([a0[0:DH_C] / a0[DH_C:DH_C + 1], a1[0:DH_C] / a1[DH_C:DH_C + 1]], axis=0)
            cols = slice(pair * LANES, (pair + 1) * LANES)
            o_ref[0, rows, cols] = (ot.T * sg_ref[0, rows, cols].astype(F32)).astype(BF16)

    tiles = [(qi, j) for qi in range(SEQ // TQ_C) for j in range(qi + 1)]
    bufs = ((sta_scr, mxa_scr), (stb_scr, mxb_scr))
    def issue(n):
        qi, j = tiles[n]
        if j == 0:
            load_queries(qi)
        scores(qi, j, *bufs[n % 2])

    issue(0)
    issue(1)
    for n, (qi, j) in enumerate(tiles):
        consume(qi, j, *bufs[n % 2])
        if j == qi:
            finish(qi)
        if n + 2 < len(tiles):
            issue(n + 2)


def _fox_attention(qt, k, ek, vt, sg, cum):
    assert TQ_C == TK_C
    ng = H_C // NH_C
    w = NH_C * DH_C
    cq = cum.transpose(0, 2, 1).reshape(BATCH, ng, NH_C, SEQ)
    blk = pl.BlockSpec((1, SEQ, w), lambda b, g: (b, 0, g))
    blk_t = pl.BlockSpec((1, w, SEQ), lambda b, g: (b, g, 0))
    return pl.pallas_call(
        _attn_kernel,
        grid=(BATCH, ng),
        in_specs=[
            blk_t, blk, blk, blk_t, blk,
            pl.BlockSpec((1, 1, NH_C, SEQ), lambda b, g: (b, g, 0, 0)),
        ],
        out_specs=blk,
        out_shape=jax.ShapeDtypeStruct((BATCH, SEQ, E_C), BF16),
        scratch_shapes=[pltpu.VMEM((NH_C, 2 * LANES, TQ_C), BF16), pltpu.VMEM((NH_C, DH_C + DEN_ROWS, TQ_C), F32),
                        pltpu.VMEM((NH_C, 1, TQ_C), F32),
                        pltpu.VMEM((NH_C, TK_C, TQ_C), F32), pltpu.VMEM((NH_C, TK_C, TQ_C), F32),
                        pltpu.VMEM((NH_C, 1, TQ_C), F32), pltpu.VMEM((NH_C, 1, TQ_C), F32)],
        compiler_params=pltpu.CompilerParams(dimension_semantics=("arbitrary", "arbitrary"),
                                             vmem_limit_bytes=VMEM_LIMIT),
        name="fox_attention",
    )(qt, k, ek, vt, sg, cq)


def _fox_mixer(x3, norm, w_in, b_f, q_norm, k_norm):
    qt, k, ek, vt, sg, cum = _fox_proj(x3, norm, w_in, b_f, q_norm, k_norm)
    return _fox_attention(qt, k, ek, vt, sg, cum)


def kernel(x, l0_norm, l0_w_in, l0_v_norm, l0_w_s, l0_b_s, l0_w_out, l1_norm, l1_w_in, l1_conv_w, l1_conv_b, l1_w_a, l1_b_a, l1_w_x, l1_b_x, l1_lam, l1_w_out, l2_norm, l2_w_in, l2_b_f, l2_q_norm, l2_k_norm, l2_w_out, l3_norm, l3_w_in, l3_v_norm, l3_w_s, l3_b_s, l3_w_out):
    n = BATCH * SEQ
    x = _gmlp_layer(x.reshape(n, D_MODEL), l0_norm, l0_w_in, l0_v_norm, l0_w_s, l0_b_s, l0_w_out)
    x = _rglru_layer(x.reshape(BATCH, SEQ, D_MODEL), l1_norm, l1_w_in, l1_conv_w, l1_conv_b, l1_w_a, l1_b_a,
                     l1_w_x, l1_b_x, l1_lam, l1_w_out)
    y = _fox_mixer(x, l2_norm, l2_w_in, l2_b_f, l2_q_norm, l2_k_norm)
    x = _gmlp_layer(x.reshape(n, D_MODEL), l3_norm, l3_w_in, l3_v_norm, l3_w_s, l3_b_s, l3_w_out,
                    pending=(y.reshape(n, E_C), l2_w_out))
    return x.reshape(BATCH, SEQ, D_MODEL)
```

```python
import functools

import jax
import jax.numpy as jnp
from jax import lax
from jax.experimental import pallas as pl
from jax.experimental.pallas import tpu as pltpu

D_MODEL = 1024
BATCH = 8
SEQ = 2048
EPS = 1e-6
CHUNK = 128
E_A = 2 * D_MODEL
G_A = 8
DG_A = E_A // G_A
E_B = 3 * D_MODEL // 2
H_B = 16
BD_B = E_B // H_B
CONV_W = 4
LRU_C = 8.0
H_C = 16
DH_C = D_MODEL // H_C
E_C = H_C * DH_C

LANES = 128
VMEM_LIMIT = 56 * 1024 * 1024

TM_A = 512
WCH_A = 1024
T_B = 64
HG_B = 4
GW_B = HG_B * BD_B
TM_C = 1024
TQ_C = 256
TK_C = 256
NH_C = 4
DEN_ROWS = 16
NEG_BIG = -1e30
LOG2E = 1.4426950408889634

F32 = jnp.float32
BF16 = jnp.bfloat16


def _dot(a, b):
    return jnp.dot(a, b, preferred_element_type=F32)


def _rms(x, g):
    ms = jnp.mean(x * x, axis=-1, keepdims=True)
    return x * lax.rsqrt(ms + EPS) * g


GELU_C = 0.7978845608028654


def _gelu_of_half(hx):
    return hx * (1.0 + jnp.tanh(hx * (2.0 * GELU_C + (8.0 * GELU_C * 0.044715) * (hx * hx))))


def _silu_of_half(hx):
    return hx * (1.0 + jnp.tanh(hx))


def _softplus(x):
    return jnp.maximum(x, 0.0) + jnp.log1p(jnp.exp(-jnp.abs(x)))


def _const_spec(shape):
    n = len(shape)
    return pl.BlockSpec(shape, lambda *_: (0,) * n, pipeline_mode=pl.Buffered(1))


def _win_chunk_copy(win_hbm, stage_scr, sem, c):
    cols = pl.ds(c * WCH_A, WCH_A)
    return pltpu.make_async_copy(win_hbm.at[:, cols], stage_scr.at[c % 2], sem.at[c % 2])


def _gmlp_kernel(has_pending, x_ref, *refs):
    if has_pending:
        yp_ref, wp_ref, *refs = refs
    nrm_ref, win_hbm, vnrm_ref, ws_ref, bst_ref, wo_ref, o_ref, vn_scr, y_scr, win_ref, stage_scr, sem = refs

    @pl.when(pl.program_id(0) == 0)
    def _():
        n_chunks = 3 * E_A // WCH_A
        _win_chunk_copy(win_hbm, stage_scr, sem, 0).start()
        for c in range(n_chunks):
            if c + 1 < n_chunks:
                _win_chunk_copy(win_hbm, stage_scr, sem, c + 1).start()
            _win_chunk_copy(win_hbm, stage_scr, sem, c).wait()
            win_ref[:, c * WCH_A:(c + 1) * WCH_A] = (0.5 * stage_scr[c % 2]).astype(BF16)

    if has_pending:
        x = x_ref[...] + _dot(yp_ref[...], wp_ref[...])
    else:
        x = x_ref[...]
    h = _rms(x, nrm_ref[...]).astype(BF16)
    v = _gelu_of_half(_dot(h, win_ref[:, E_A:2 * E_A]))
    vn_scr[...] = (v * vnrm_ref[...]).astype(BF16)
    rs = lax.rsqrt(jnp.mean(v * v, axis=-1, keepdims=True) + EPS)
    col_scale = [jnp.broadcast_to(rs[c * CHUNK:(c + 1) * CHUNK], (CHUNK, CHUNK)).T for c in range(TM_A // CHUNK)]
    row = lax.broadcasted_iota(jnp.int32, (CHUNK, CHUNK), 0)
    col = lax.broadcasted_iota(jnp.int32, (CHUNK, CHUNK), 1)
    causal = col <= row
    for g in range(G_A):
        c0 = g * DG_A
        u = _gelu_of_half(_dot(h, win_ref[:, c0:c0 + DG_A]))
        gate = _silu_of_half(_dot(h, win_ref[:, 2 * E_A + c0:2 * E_A + c0 + DG_A]))
        wm = jnp.where(causal, ws_ref[g], 0.0)
        bias = bst_ref[:, g:g + 1]
        for c in range(TM_A // CHUNK):
            r0 = c * CHUNK
            w = (wm * col_scale[c]).astype(BF16)
            mixed = _dot(w, vn_scr[r0:r0 + CHUNK, c0:c0 + DG_A]) + bias
            y = u[r0:r0 + CHUNK] * mixed * gate[r0:r0 + CHUNK]
            y_scr[r0:r0 + CHUNK, c0:c0 + DG_A] = y.astype(BF16)
    o_ref[...] = x + _dot(y_scr[...], wo_ref[...])


def _gmlp_layer(x2, norm, w_in, v_norm, w_s, b_s, w_out, pending=None):
    n = x2.shape[0]
    rows = pl.BlockSpec((TM_A, D_MODEL), lambda i: (i, 0))
    pending_specs, pending_args = [], []
    if pending is not None:
        y_prev, w_prev = pending
        pending_specs = [pl.BlockSpec((TM_A, y_prev.shape[1]), lambda i: (i, 0)), _const_spec(w_prev.shape)]
        pending_args = [y_prev, w_prev.astype(BF16)]
    return pl.pallas_call(
        functools.partial(_gmlp_kernel, pending is not None),
        grid=(n // TM_A,),
        in_specs=[
            rows,
            *pending_specs,
            _const_spec((1, D_MODEL)),
            pl.BlockSpec(memory_space=pl.ANY),
            _const_spec((1, E_A)),
            _const_spec((G_A, CHUNK, CHUNK)),
            _const_spec((CHUNK, G_A)),
            _const_spec((E_A, D_MODEL)),
        ],
        out_specs=rows,
        out_shape=jax.ShapeDtypeStruct((n, D_MODEL), F32),
        scratch_shapes=[pltpu.VMEM((TM_A, E_A), BF16), pltpu.VMEM((TM_A, E_A), BF16),
                        pltpu.VMEM((D_MODEL, 3 * E_A), BF16), pltpu.VMEM((2, D_MODEL, WCH_A), F32),
                        pltpu.SemaphoreType.DMA((2,))],
        compiler_params=pltpu.CompilerParams(dimension_semantics=("arbitrary",), vmem_limit_bytes=VMEM_LIMIT),
        name="gmlp_layer",
    )(x2, *pending_args, norm.reshape(1, D_MODEL), w_in, v_norm.reshape(1, E_A), w_s, b_s.T, w_out.astype(BF16))


ROWS_B = T_B * BATCH
TAIL_B = (CONV_W - 1) * BATCH


def _rglru_kernel(x_ref, nrm_ref, win_ref, cw_ref, cb_ref, wbd_ref, hba_ref, hbx_ref, lam_ref, wo_ref, o_ref,
                  hn_scr, xb_scr, a_scr, hs_scr, state_scr, out_scr):
    @pl.when(pl.program_id(0) == 0)
    def _():
        xb_scr[0:TAIL_B, :] = jnp.zeros((TAIL_B, E_B), F32)
        state_scr[...] = jnp.zeros((BATCH, E_B), F32)

    nrm = nrm_ref[...]
    for b in range(BATCH):
        hb = _rms(x_ref[b], nrm)
        for c in range(D_MODEL // LANES):
            hn_scr[c, pl.ds(b, T_B, stride=BATCH), :] = hb[:, c * LANES:(c + 1) * LANES]
    h = jnp.concatenate([hn_scr[c] for c in range(D_MODEL // LANES)], axis=1).astype(BF16)
    xb_scr[TAIL_B:TAIL_B + ROWS_B, :] = _dot(h, win_ref[:, 0:E_B])
    gate = _silu_of_half(_dot(h, win_ref[:, E_B:2 * E_B]))

    xc = cb_ref[...] + cw_ref[0:1, :] * xb_scr[0:ROWS_B, :]
    for k in range(1, CONV_W):
        xc = xc + cw_ref[k:k + 1, :] * xb_scr[k * BATCH:k * BATCH + ROWS_B, :]
    tail = xb_scr[ROWS_B:ROWS_B + TAIL_B, :]
    xb_scr[0:TAIL_B, :] = tail
    xcb = xc.astype(BF16)

    quarter_c_sp = (0.25 * LRU_C) * _softplus(-lam_ref[...])
    for j in range(E_B // GW_B):
        c0 = j * GW_B
        pre = _dot(xcb[:, c0:c0 + GW_B], wbd_ref[j])
        ur = jnp.tanh(pre[:, 0:GW_B] + hba_ref[:, c0:c0 + GW_B])
        ui = jnp.tanh(pre[:, GW_B:2 * GW_B] + hbx_ref[:, c0:c0 + GW_B])
        tn = jnp.tanh(quarter_c_sp[:, c0:c0 + GW_B] * (1.0 + ur))
        d = 1.0 / (1.0 + tn)
        a_scr[:, c0:c0 + GW_B] = (1.0 - tn) * d
        root = jnp.where(tn <= 0.0, 0.0, tn * lax.rsqrt(tn))
        hs_scr[:, c0:c0 + GW_B] = (d * root) * ((1.0 + ui) * xc[:, c0:c0 + GW_B])

    hcur = state_scr[...]
    for t in range(T_B):
        rows = slice(t * BATCH, (t + 1) * BATCH)
        hcur = a_scr[rows, :] * hcur + hs_scr[rows, :]
        hs_scr[rows, :] = hcur
    state_scr[...] = hcur

    y = (hs_scr[...] * gate).astype(BF16)
    out = _dot(y, wo_ref[...])
    for c in range(D_MODEL // LANES):
        out_scr[c] = out[:, c * LANES:(c + 1) * LANES]
    for b in range(BATCH):
        ob = jnp.concatenate([out_scr[c, pl.ds(b, T_B, stride=BATCH), :] for c in range(D_MODEL // LANES)], axis=1)
        o_ref[b] = x_ref[b] + ob


def _block_diag_gates(w_a, w_x):
    def bd(w):
        wg = w.reshape(H_B // HG_B, HG_B, BD_B, BD_B)
        eye = jnp.eye(HG_B, dtype=w.dtype)
        return jnp.einsum('ghij,hk->ghikj', wg, eye).reshape(H_B // HG_B, GW_B, GW_B)
    return jnp.concatenate([bd(w_a), bd(w_x)], axis=-1)


def _rglru_layer(x3, norm, w_in, conv_w, conv_b, w_a, b_a, w_x, b_x, lam, w_out):
    ng = H_B // HG_B
    w_half = jnp.concatenate([w_in[:, 0:E_B].astype(BF16), (0.5 * w_in[:, E_B:2 * E_B]).astype(BF16)], axis=1)
    return pl.pallas_call(
        _rglru_kernel,
        grid=(SEQ // T_B,),
        in_specs=[
            pl.BlockSpec((BATCH, T_B, D_MODEL), lambda i: (0, i, 0)),
            _const_spec((1, D_MODEL)),
            _const_spec((D_MODEL, 2 * E_B)),
            _const_spec((CONV_W, E_B)),
            _const_spec((1, E_B)),
            _const_spec((ng, GW_B, 2 * GW_B)),
            _const_spec((1, E_B)),
            _const_spec((1, E_B)),
            _const_spec((1, E_B)),
            _const_spec((E_B, D_MODEL)),
        ],
        out_specs=pl.BlockSpec((BATCH, T_B, D_MODEL), lambda i: (0, i, 0)),
        out_shape=jax.ShapeDtypeStruct((BATCH, SEQ, D_MODEL), F32),
        scratch_shapes=[
            pltpu.VMEM((D_MODEL // LANES, ROWS_B, LANES), F32),
            pltpu.VMEM((TAIL_B + ROWS_B, E_B), F32),
            pltpu.VMEM((ROWS_B, E_B), F32),
            pltpu.VMEM((ROWS_B, E_B), F32),
            pltpu.VMEM((BATCH, E_B), F32),
            pltpu.VMEM((D_MODEL // LANES, ROWS_B, LANES), F32),
        ],
        compiler_params=pltpu.CompilerParams(dimension_semantics=("arbitrary",), vmem_limit_bytes=VMEM_LIMIT),
        name="rglru_layer",
    )(x3, norm.reshape(1, D_MODEL), w_half, conv_w, conv_b.reshape(1, E_B),
      (0.5 * _block_diag_gates(w_a, w_x)).astype(BF16), (0.5 * b_a).reshape(1, E_B), (0.5 * b_x).reshape(1, E_B),
      lam.reshape(1, E_B), w_out.astype(BF16))


N_SPLIT = 3


def _fox_proj_kernel(x_ref, nrm_ref, wkg_ref, wqvt_ref, wf_ref, bf_ref, kg_ref, seg_ref, segt_ref, place_ref, tri_ref,
                     qt_ref, k_ref, ek_ref, vt_ref, sg_ref, cum_ref, carry_scr):
    @pl.when(pl.program_id(1) == 0)
    def _():
        carry_scr[...] = jnp.zeros((1, LANES), F32)

    h = _rms(x_ref[0], nrm_ref[...]).astype(BF16)

    zk = _dot(h, wkg_ref[:, 0:E_C])
    ss = _dot((zk * zk).astype(BF16), seg_ref[...])
    r = lax.rsqrt(ss * (1.0 / DH_C) + EPS)
    r_hi = r.astype(BF16)
    r_lo = (r - r_hi.astype(F32)).astype(BF16)
    rexp = _dot(jnp.concatenate([r_hi, r_lo], axis=1), segt_ref[...])
    k_ref[0] = (zk * rexp * kg_ref[...]).astype(BF16)
    sg_ref[0] = _silu_of_half(_dot(h, wkg_ref[:, E_C:2 * E_C])).astype(BF16)

    zqv = lax.dot_general(wqvt_ref[...], h, (((1,), (1,)), ((), ())), preferred_element_type=F32)
    vt_ref[0] = zqv[E_C:2 * E_C].astype(BF16)
    for hd in range(H_C):
        rows = slice(hd * DH_C, (hd + 1) * DH_C)
        zq = zqv[rows]
        rq = lax.rsqrt(jnp.mean(zq * zq, axis=0, keepdims=True) + EPS)
        qt_ref[0, rows, :] = (zq * rq).astype(BF16)

    z = _dot(h, wf_ref[...]) + bf_ref[...]
    log_f = jnp.minimum(z, 0.0) - jnp.log1p(jnp.exp(-jnp.abs(z)))
    group = lax.broadcasted_iota(jnp.int32, (TM_C, LANES), 1) // H_C

    def split3(v):
        hi = v.astype(BF16).astype(F32)
        mid = (v - hi).astype(BF16).astype(F32)
        lo = v - hi - mid
        return jnp.where(group == 0, hi, jnp.where(group == 1, mid, lo)).astype(BF16)

    cp = _dot(tri_ref[...], split3(log_f))
    cum = cp + pltpu.roll(cp, LANES - H_C, axis=1) + pltpu.roll(cp, LANES - 2 * H_C, axis=1) + carry_scr[...]
    carry_scr[...] = cum[TM_C - 1:TM_C, :]
    c2 = cum * LOG2E
    cum_ref[0] = c2[:, 0:H_C]
    c2 = jnp.where(group == 0, c2,
                   jnp.where(group == 1, pltpu.roll(c2, H_C, axis=1), pltpu.roll(c2, 2 * H_C, axis=1)))
    ek_ref[0] = _dot(split3(c2), place_ref[...]).astype(BF16)


def _fox_proj(x3, norm, w_in, b_f, q_norm, k_norm):
    w_kg = jnp.concatenate([w_in[:, E_C:2 * E_C].astype(BF16), (0.5 * w_in[:, 3 * E_C:4 * E_C]).astype(BF16)], axis=1)
    w_qvt = jnp.concatenate([w_in[:, 0:E_C].astype(BF16).T, w_in[:, 2 * E_C:3 * E_C].astype(BF16).T], axis=0)
    pad = LANES - N_SPLIT * H_C
    w_f = jnp.pad(jnp.tile(w_in[:, 4 * E_C:], (1, N_SPLIT)), ((0, 0), (0, pad))).astype(BF16)
    b_fp = jnp.pad(jnp.tile(b_f, N_SPLIT), (0, pad)).reshape(1, LANES)
    kg = jnp.tile(k_norm * q_norm, H_C).reshape(1, E_C) * (DH_C ** -0.5 * LOG2E)
    head_of = jnp.arange(E_C) // DH_C
    seg = (head_of[:, None] == jnp.arange(LANES)[None, :]).astype(BF16)
    segt2 = jnp.concatenate([seg.T, seg.T], axis=0)
    src = jnp.arange(LANES)
    dst = (src % H_C) * DH_C + src // H_C
    place = jnp.where((src[:, None] < N_SPLIT * H_C) & (dst[:, None] == jnp.arange(E_C)[None, :]), -1.0, 0.0)
    tok = jax.ShapeDtypeStruct((BATCH, SEQ, E_C), BF16)
    tok_t = jax.ShapeDtypeStruct((BATCH, E_C, SEQ), BF16)
    blk = pl.BlockSpec((1, TM_C, E_C), lambda b, i: (b, i, 0))
    blk_t = pl.BlockSpec((1, E_C, TM_C), lambda b, i: (b, 0, i))
    return pl.pallas_call(
        _fox_proj_kernel,
        grid=(BATCH, SEQ // TM_C),
        in_specs=[
            pl.BlockSpec((1, TM_C, D_MODEL), lambda b, i: (b, i, 0)),
            _const_spec((1, D_MODEL)),
            _const_spec((D_MODEL, 2 * E_C)),
            _const_spec((2 * E_C, D_MODEL)),
            _const_spec((D_MODEL, LANES)),
            _const_spec((1, LANES)),
            _const_spec((1, E_C)),
            _const_spec((E_C, LANES)),
            _const_spec((2 * LANES, E_C)),
            _const_spec((LANES, E_C)),
            _const_spec((TM_C, TM_C)),
        ],
        out_specs=[blk_t, blk, blk, blk_t, blk, pl.BlockSpec((1, TM_C, H_C), lambda b, i: (b, i, 0))],
        out_shape=[tok_t, tok, tok, tok_t, tok, jax.ShapeDtypeStruct((BATCH, SEQ, H_C), F32)],
        scratch_shapes=[pltpu.VMEM((1, LANES), F32)],
        compiler_params=pltpu.CompilerParams(dimension_semantics=("arbitrary", "arbitrary"),
                                             vmem_limit_bytes=VMEM_LIMIT),
        name="fox_proj",
    )(x3, norm.reshape(1, D_MODEL), w_kg, w_qvt, w_f, b_fp, kg, seg, segt2, place.astype(BF16),
      jnp.tril(jnp.ones((TM_C, TM_C), BF16)))


def _attn_kernel(qt_ref, k_ref, ek_ref, vt_ref, sg_ref, cq_ref, o_ref,
                 qa_scr, acc_scr, m_scr, sta_scr, stb_scr, mxa_scr, mxb_scr):
    causal = (lax.broadcasted_iota(jnp.int32, (TK_C, TQ_C), 0) <= lax.broadcasted_iota(jnp.int32, (TK_C, TQ_C), 1))
    ones_v = jnp.ones((DEN_ROWS, TK_C), BF16)
    zero_q = jnp.zeros((DH_C, TQ_C), BF16)
    row_q = lax.broadcasted_iota(jnp.int32, (LANES, TQ_C), 0)
    heads = range(NH_C)

    for hh in heads:
        odd = hh % 2
        qa_scr[hh, (1 - odd) * DH_C:(2 - odd) * DH_C, :] = zero_q
        mine = jnp.logical_and(row_q >= odd * DH_C, row_q < odd * DH_C + N_SPLIT)
        qa_scr[hh, LANES:2 * LANES, :] = jnp.where(mine, 1.0, 0.0).astype(BF16)

    def load_queries(qi):
        cols = slice(qi * TQ_C, (qi + 1) * TQ_C)
        for hh in heads:
            odd = hh % 2
            qa_scr[hh, odd * DH_C:(odd + 1) * DH_C, :] = qt_ref[0, hh * DH_C:(hh + 1) * DH_C, cols]

    def scores(qi, j, st_ref, mx_ref):
        keys = slice(j * TK_C, (j + 1) * TK_C)
        for hh in heads:
            pair = hh // 2
            ka = jnp.concatenate([k_ref[0, keys, pair * LANES:(pair + 1) * LANES],
                                  ek_ref[0, keys, pair * LANES:(pair + 1) * LANES]], axis=1)
            st = _dot(ka, qa_scr[hh])
            if j == qi:
                st = jnp.where(causal, st, NEG_BIG)
            st_ref[hh] = st
            mx_ref[hh] = jnp.max(st, axis=0, keepdims=True)

    def consume(qi, j, st_ref, mx_ref):
        alpha, pt = [], []
        for hh in heads:
            cq = cq_ref[0, 0, hh:hh + 1, qi * TQ_C:(qi + 1) * TQ_C]
            if j == 0:
                m = mx_ref[hh] + cq
            else:
                m_old = m_scr[hh]
                m = jnp.maximum(m_old, mx_ref[hh] + cq)
                alpha.append(jnp.exp2(m_old - m))
            pt.append(jnp.exp2((st_ref[hh] - (m - cq)).astype(BF16)))
            m_scr[hh] = m
        pv = []
        for hh in heads:
            vt = vt_ref[0, hh * DH_C:(hh + 1) * DH_C, j * TK_C:(j + 1) * TK_C]
            pv.append(_dot(jnp.concatenate([vt, ones_v], axis=0), pt[hh]))
        for hh in heads:
            acc_scr[hh] = pv[hh] if j == 0 else alpha[hh] * acc_scr[hh] + pv[hh]

    def finish(qi):
        rows = slice(qi * TQ_C, (qi + 1) * TQ_C)
        for pair in range(NH_C // 2):
            a0 = acc_scr[2 * pair]
            a1 = acc_scr[2 * pair + 1]
            ot = jnp.concatenate([a0[0:DH_C] / a0[DH_C:DH_C + 1], a1[0:DH_C] / a1[DH_C:DH_C + 1]], axis=0)
            cols = slice(pair * LANES, (pair + 1) * LANES)
            o_ref[0, rows, cols] = (ot.T * sg_ref[0, rows, cols].astype(F32)).astype(BF16)

    tiles = [(qi, j) for qi in range(SEQ // TQ_C) for j in range(qi + 1)]
    bufs = ((sta_scr, mxa_scr), (stb_scr, mxb_scr))
    def issue(n):
        qi, j = tiles[n]
        if j == 0:
            load_queries(qi)
        scores(qi, j, *bufs[n % 2])

    issue(0)
    issue(1)
    for n, (qi, j) in enumerate(tiles):
        consume(qi, j, *bufs[n % 2])
        if j == qi:
            finish(qi)
        if n + 2 < len(tiles):
            issue(n + 2)


def _fox_attention(qt, k, ek, vt, sg, cum):
    assert TQ_C == TK_C
    ng = H_C // NH_C
    w = NH_C * DH_C
    cq = cum.transpose(0, 2, 1).reshape(BATCH, ng, NH_C, SEQ)
    blk = pl.BlockSpec((1, SEQ, w), lambda b, g: (b, 0, g))
    blk_t = pl.BlockSpec((1, w, SEQ), lambda b, g: (b, g, 0))
    return pl.pallas_call(
        _attn_kernel,
        grid=(BATCH, ng),
        in_specs=[
            blk_t, blk, blk, blk_t, blk,
            pl.BlockSpec((1, 1, NH_C, SEQ), lambda b, g: (b, g, 0, 0)),
        ],
        out_specs=blk,
        out_shape=jax.ShapeDtypeStruct((BATCH, SEQ, E_C), BF16),
        scratch_shapes=[pltpu.VMEM((NH_C, 2 * LANES, TQ_C), BF16), pltpu.VMEM((NH_C, DH_C + DEN_ROWS, TQ_C), F32),
                        pltpu.VMEM((NH_C, 1, TQ_C), F32),
                        pltpu.VMEM((NH_C, TK_C, TQ_C), F32), pltpu.VMEM((NH_C, TK_C, TQ_C), F32),
                        pltpu.VMEM((NH_C, 1, TQ_C), F32), pltpu.VMEM((NH_C, 1, TQ_C), F32)],
        compiler_params=pltpu.CompilerParams(dimension_semantics=("arbitrary", "arbitrary"),
                                             vmem_limit_bytes=VMEM_LIMIT),
        name="fox_attention",
    )(qt, k, ek, vt, sg, cq)


def _fox_mixer(x3, norm, w_in, b_f, q_norm, k_norm):
    qt, k, ek, vt, sg, cum = _fox_proj(x3, norm, w_in, b_f, q_norm, k_norm)
    return _fox_attention(qt, k, ek, vt, sg, cum)


def kernel(x, l0_norm, l0_w_in, l0_v_norm, l0_w_s, l0_b_s, l0_w_out, l1_norm, l1_w_in, l1_conv_w, l1_conv_b, l1_w_a, l1_b_a, l1_w_x, l1_b_x, l1_lam, l1_w_out, l2_norm, l2_w_in, l2_b_f, l2_q_norm, l2_k_norm, l2_w_out, l3_norm, l3_w_in, l3_v_norm, l3_w_s, l3_b_s, l3_w_out):
    n = BATCH * SEQ
    x = _gmlp_layer(x.reshape(n, D_MODEL), l0_norm, l0_w_in, l0_v_norm, l0_w_s, l0_b_s, l0_w_out)
    x = _rglru_layer(x.reshape(BATCH, SEQ, D_MODEL), l1_norm, l1_w_in, l1_conv_w, l1_conv_b, l1_w_a, l1_b_a,
                     l1_w_x, l1_b_x, l1_lam, l1_w_out)
    y = _fox_mixer(x, l2_norm, l2_w_in, l2_b_f, l2_q_norm, l2_k_norm)
    x = _gmlp_layer(x.reshape(n, D_MODEL), l3_norm, l3_w_in, l3_v_norm, l3_w_s, l3_b_s, l3_w_out,
                    pending=(y.reshape(n, E_C), l2_w_out))
    return x.reshape(BATCH, SEQ, D_MODEL)
```
